```python
import math
import jax, jax.numpy as jnp
from jax import lax
import numpy as np

D_MODEL = 1024
BATCH = 1
SEQ = 16384
DEPTH = 2
DEC_BATCH = 32
DEC_SEQ = 1
PAST_LEN = 16384
PAGE_SIZE = 128

N_A = DEPTH // 2
N_B = DEPTH - N_A
EPS = 1e-6
NEG = -1e30
CHUNK = 128
D_SGU = D_MODEL
SGU_GROUPS = 8
SGU_GW = D_SGU // SGU_GROUPS
HEAD_DIM = 64
N_HEADS_B = 8
WINDOWS = (128, 512, 2048)
DILATIONS = (1, 4, 16)
N_GROUPS_B = len(WINDOWS)
Q_WIDTH = N_GROUPS_B * N_HEADS_B * HEAD_DIM
KV_WIDTH = 2 * Q_WIDTH
PEER_HEADS = 8
PEER_NKEYS = 128
PEER_N_EXPERTS = PEER_NKEYS * PEER_NKEYS
PEER_DKEY = 256
PEER_TOPK = 16
PEER_BLOCK = 256
D_PLE = 256

kernel_name = "yoco_sgu_dilated_peer_step"


def rmsnorm(x, g):
    xf = x.astype(jnp.float32)
    y = xf * lax.rsqrt(jnp.mean(xf * xf, axis=-1, keepdims=True) + EPS)
    return (y * g.astype(jnp.float32)).astype(x.dtype)


def alibi_slopes():
    n = N_GROUPS_B * N_HEADS_B
    e = (jnp.arange(n, dtype=jnp.float32) + 1.0) * (8.0 / n)
    return jnp.exp2(-e).reshape(N_GROUPS_B, N_HEADS_B)


def sgu_mixer(xn, w_in, g_v, w_s, b_s, w_out):
    B, T, _ = xn.shape
    uv = jax.nn.gelu(xn @ w_in, approximate=False)
    u, v = uv[..., :D_SGU], uv[..., D_SGU:]
    v = rmsnorm(v, g_v)
    t_pad = -(-T // CHUNK) * CHUNK
    vp = jnp.pad(v, ((0, 0), (0, t_pad - T), (0, 0))).reshape(B, t_pad // CHUNK, CHUNK, SGU_GROUPS, SGU_GW)
    causal = jnp.tril(jnp.ones((CHUNK, CHUNK), dtype=bool))
    ws = jnp.where(causal[None], w_s, jnp.zeros_like(w_s))
    mixed = jnp.einsum('gij,bnjgc->bnigc', ws, vp) + b_s.T[None, None, :, :, None]
    mixed = mixed.reshape(B, t_pad, D_SGU)[:, :T]
    return (u * mixed) @ w_out, v


def dilated_attn_full(q, k, v, slopes, window, dilation):
    B, S, H, dh = q.shape
    r = dilation
    span = window // dilation
    unit = r * span
    s_pad = -(-S // unit) * unit
    padw = ((0, 0), (0, s_pad - S), (0, 0), (0, 0))
    q, k, v = jnp.pad(q, padw), jnp.pad(k, padw), jnp.pad(v, padw)
    L = s_pad // r
    nb = L // span

    def strided(t):
        return t.reshape(B, L, r, H, -1).transpose(0, 2, 1, 3, 4).reshape(B, r, nb, span, H, -1)

    def with_prev(t):
        prev = jnp.pad(t, ((0, 0), (0, 0), (1, 0), (0, 0), (0, 0), (0, 0)))[:, :, :-1]
        return jnp.concatenate([prev, t], axis=3)

    def unstrided(t):
        t = t.reshape(B, r, L, H, -1).transpose(0, 2, 1, 3, 4).reshape(B, s_pad, H, -1)
        return t[:, :S]

    qs = strided(q).astype(jnp.float32)
    kb = with_prev(strided(k)).astype(jnp.float32)
    vb = with_prev(strided(v)).astype(jnp.float32)
    s = jnp.einsum('brnqhd,brnkhd->brnhqk', qs, kb) * (dh ** -0.5)
    qi = jnp.arange(span)
    ki = jnp.arange(2 * span) - span
    dist = qi[:, None] - ki[None, :]
    first = (jnp.arange(nb) == 0)[:, None, None] & (ki < 0)[None, None, :]
    valid = ((dist >= 0) & (dist <= span))[None] & jnp.logical_not(first)
    s = s - slopes[:, None, None] * (dist * r).astype(jnp.float32)[None]
    s = jnp.where(valid[None, None, :, None], s, NEG)
    m = jnp.max(s, axis=-1, keepdims=True)
    e = jnp.exp(s - m)
    l = jnp.sum(e, axis=-1, keepdims=True)
    o = jnp.einsum('brnhqk,brnkhd->brnqhd', e, vb) / l.transpose(0, 1, 2, 4, 3, 5)
    lse = (m + jnp.log(l)).transpose(0, 1, 2, 4, 3, 5)
    return unstrided(o), unstrided(lse)[..., 0]


def dilated_attn_step(q, k_full, v_full, slopes, window, dilation):
    N, T, H, dh = q.shape
    Lb = k_full.shape[1] - T
    span = window // dilation
    j = jnp.arange(span + 1)
    idx = (Lb + jnp.arange(T))[:, None] - j[None, :] * dilation
    valid = idx >= 0
    idx = jnp.maximum(idx, 0)
    kg = k_full[:, idx].astype(jnp.float32)
    vg = v_full[:, idx].astype(jnp.float32)
    s = jnp.einsum('nthd,ntjhd->nhtj', q.astype(jnp.float32), kg) * (dh ** -0.5)
    s = s - slopes[:, None, None] * (j * dilation).astype(jnp.float32)[None, None, :]
    s = jnp.where(valid[None, None], s, NEG)
    m = jnp.max(s, axis=-1, keepdims=True)
    e = jnp.exp(s - m)
    l = jnp.sum(e, axis=-1, keepdims=True)
    o = jnp.einsum('nhtj,ntjhd->nthd', e, vg) / l.transpose(0, 2, 1, 3)
    lse = (m + jnp.log(l))[..., 0].transpose(0, 2, 1)
    return o, lse


def peer(xn, w_query, subkeys, expert_u, expert_v):
    B, T, D = xn.shape
    n = B * T
    x2 = xn.reshape(n, D)
    q = (x2 @ w_query).reshape(n, PEER_HEADS, 2, PEER_DKEY // 2).astype(jnp.float32)
    sc = jnp.einsum('nhcd,hckd->nhck', q, subkeys.astype(jnp.float32))
    s1, i1 = lax.top_k(sc[:, :, 0], PEER_TOPK)
    s2, i2 = lax.top_k(sc[:, :, 1], PEER_TOPK)
    n_cand = PEER_TOPK * PEER_TOPK
    cand = (s1[..., :, None] + s2[..., None, :]).reshape(n, PEER_HEADS, n_cand)
    cidx = (i1[..., :, None] * PEER_NKEYS + i2[..., None, :]).reshape(n, PEER_HEADS, n_cand)
    top, pos = lax.top_k(cand, PEER_TOPK)
    eidx = jnp.take_along_axis(cidx, pos, axis=-1).reshape(n, PEER_HEADS * PEER_TOPK)
    gate = jax.nn.softmax(top, axis=-1).reshape(n, PEER_HEADS * PEER_TOPK)
    blk = min(PEER_BLOCK, n)
    n_pad = -(-n // blk) * blk
    pad = n_pad - n
    nblk = n_pad // blk
    xb = jnp.pad(x2, ((0, pad), (0, 0))).reshape(nblk, blk, D)
    eb = jnp.pad(eidx, ((0, pad), (0, 0))).reshape(nblk, blk, PEER_HEADS * PEER_TOPK)
    gb = jnp.pad(gate, ((0, pad), (0, 0))).reshape(nblk, blk, PEER_HEADS * PEER_TOPK)

    def expert_block(args):
        xs, es, gs = args
        u = jnp.take(expert_u, es, axis=0)
        act = jax.nn.gelu(jnp.einsum('bkd,bd->bk', u, xs), approximate=False)
        v = jnp.take(expert_v, es, axis=0)
        w = (gs * act.astype(jnp.float32)).astype(xs.dtype)
        return jnp.einsum('bk,bkd->bd', w, v)

    out = lax.map(expert_block, (xb, eb, gb)).reshape(n_pad, D)[:n]
    return out.reshape(B, T, D)


def run_trunk(x, p, kv_bufs, prm):
    B, T, _ = x.shape
    slopes = alibi_slopes()
    h = x
    v_rows = []
    kv_full = None
    new_kv = []
    for i in range(DEPTH):
        xn = rmsnorm(h, prm['g_mix'][i])
        if i < N_A:
            mix, v = sgu_mixer(xn, prm['sgu_w_in'][i], prm['sgu_g_v'][i], prm['sgu_w_s'][i],
                               prm['sgu_b_s'][i], prm['sgu_w_out'][i])
            v_rows.append(v)
        else:
            b = i - N_A
            if kv_full is None:
                kv = (rmsnorm(h, prm['kv_g']) @ prm['w_kv']).reshape(B, T, N_GROUPS_B, 2, N_HEADS_B, HEAD_DIM)
                kv_full = []
                for g in range(N_GROUPS_B):
                    kg = kv[:, :, g]
                    full = kg if kv_bufs is None else jnp.concatenate([kv_bufs[g], kg], axis=1)
                    keep = min(WINDOWS[g], full.shape[1])
                    kv_full.append(full)
                    new_kv.append(full[:, full.shape[1] - keep:])
            q = (xn @ prm['attn_w_q'][b]).reshape(B, T, N_GROUPS_B, N_HEADS_B, HEAD_DIM)
            outs, lses = [], []
            for g in range(N_GROUPS_B):
                full = kv_full[g]
                if kv_bufs is None:
                    o, lse = dilated_attn_full(q[:, :, g], full[:, :, 0], full[:, :, 1], slopes[g], WINDOWS[g], DILATIONS[g])
                else:
                    o, lse = dilated_attn_step(q[:, :, g], full[:, :, 0], full[:, :, 1], slopes[g], WINDOWS[g], DILATIONS[g])
                outs.append(o)
                lses.append(lse)
            wts = jax.nn.softmax(jnp.stack(lses, axis=0), axis=0)
            o = jnp.einsum('gbth,gbthd->bthd', wts, jnp.stack(outs, axis=0)).astype(x.dtype)
            mix = o.reshape(B, T, N_HEADS_B * HEAD_DIM) @ prm['attn_w_o'][b]
        h = h + mix
        h = h + peer(rmsnorm(h, prm['g_ffn'][i]), prm['peer_w_query'][i], prm['peer_subkeys'][i],
                     prm['peer_u'][i], prm['peer_v'][i])
        gate = jax.nn.sigmoid(rmsnorm(h, prm['ple_g'][i]) @ prm['ple_w_gate'][i])
        h = h + gate * (p[i] @ prm['ple_w_proj'][i])
    y = rmsnorm(h, prm['g_final'])
    return y, jnp.stack(v_rows, axis=0), new_kv


def setup_inputs(seed: int = 0) -> dict:
    key = jax.random.key(seed)
    ks = jax.random.split(key, 32)
    f32 = jnp.float32

    def nrm(k, shape, scale):
        return jax.random.normal(k, shape, f32) * scale

    def gain(k, shape):
        return 1.0 + 0.05 * jax.random.normal(k, shape, f32)

    buf = [min(w, PAST_LEN) for w in WINDOWS]
    D = D_MODEL
    return {
        'x_prompt': nrm(ks[0], (BATCH, SEQ, D), 1.0),
        'x_sample': nrm(ks[1], (DEC_BATCH, DEC_SEQ, D), 1.0),
        'state_kv_w128': nrm(ks[2], (DEC_BATCH, buf[0], 2, N_HEADS_B, HEAD_DIM), 1.0),
        'state_kv_w512': nrm(ks[3], (DEC_BATCH, buf[1], 2, N_HEADS_B, HEAD_DIM), 1.0),
        'state_kv_w2048': nrm(ks[4], (DEC_BATCH, buf[2], 2, N_HEADS_B, HEAD_DIM), 1.0),
        'p_prompt': nrm(ks[5], (DEPTH, BATCH, SEQ, D_PLE), 1.0),
        'p_sample': nrm(ks[6], (DEPTH, DEC_BATCH, DEC_SEQ, D_PLE), 1.0),
        'g_mix': gain(ks[7], (DEPTH, D)),
        'sgu_w_in': nrm(ks[8], (N_A, D, 2 * D_SGU), D ** -0.5),
        'sgu_g_v': gain(ks[9], (N_A, D_SGU)),
        'sgu_w_s': nrm(ks[10], (N_A, SGU_GROUPS, CHUNK, CHUNK), CHUNK ** -0.5),
        'sgu_b_s': gain(ks[11], (N_A, SGU_GROUPS, CHUNK)),
        'sgu_w_out': nrm(ks[12], (N_A, D_SGU, D), 0.5 * D_SGU ** -0.5),
        'kv_g': gain(ks[13], (D,)),
        'w_kv': nrm(ks[14], (D, KV_WIDTH), D ** -0.5),
        'attn_w_q': nrm(ks[15], (N_B, D, Q_WIDTH), D ** -0.5),
        'attn_w_o': nrm(ks[16], (N_B, N_HEADS_B * HEAD_DIM, D), 0.5 * (N_HEADS_B * HEAD_DIM) ** -0.5),
        'g_ffn': gain(ks[17], (DEPTH, D)),
        'peer_w_query': nrm(ks[18], (DEPTH, D, PEER_HEADS * PEER_DKEY), D ** -0.5),
        'peer_subkeys': nrm(ks[19], (DEPTH, PEER_HEADS, 2, PEER_NKEYS, PEER_DKEY // 2), (PEER_DKEY // 2) ** -0.5),
        'peer_u': nrm(ks[20], (DEPTH, PEER_N_EXPERTS, D), D ** -0.5),
        'peer_v': nrm(ks[21], (DEPTH, PEER_N_EXPERTS, D), PEER_HEADS ** -0.5),
        'ple_g': gain(ks[22], (DEPTH, D)),
        'ple_w_gate': nrm(ks[23], (DEPTH, D, D), D ** -0.5),
        'ple_w_proj': nrm(ks[24], (DEPTH, D_PLE, D), 0.5 * D_PLE ** -0.5),
        'g_final': gain(ks[25], (D,)),
    }


def reference(x_prompt, x_sample, state_kv_w128, state_kv_w512, state_kv_w2048, p_prompt, p_sample,
              g_mix, sgu_w_in, sgu_g_v, sgu_w_s, sgu_b_s, sgu_w_out, kv_g, w_kv, attn_w_q, attn_w_o,
              g_ffn, peer_w_query, peer_subkeys, peer_u, peer_v, ple_g, ple_w_gate, ple_w_proj, g_final):
    prm = dict(g_mix=g_mix, sgu_w_in=sgu_w_in, sgu_g_v=sgu_g_v, sgu_w_s=sgu_w_s, sgu_b_s=sgu_b_s,
               sgu_w_out=sgu_w_out, kv_g=kv_g, w_kv=w_kv, attn_w_q=attn_w_q, attn_w_o=attn_w_o,
               g_ffn=g_ffn, peer_w_query=peer_w_query, peer_subkeys=peer_subkeys, peer_u=peer_u,
               peer_v=peer_v, ple_g=ple_g, ple_w_gate=ple_w_gate, ple_w_proj=ple_w_proj, g_final=g_final)
    y_prompt, _, kv_p = run_trunk(x_prompt, p_prompt, None, prm)
    y_sample, sgu_v_sample, kv_s = run_trunk(x_sample, p_sample, (state_kv_w128, state_kv_w512, state_kv_w2048), prm)
    return (y_prompt, y_sample, sgu_v_sample, kv_p[0], kv_p[1], kv_p[2], kv_s[0], kv_s[1], kv_s[2])
```

```python
import functools
import math

import numpy as np
import jax
import jax.numpy as jnp
from jax import lax
from jax.experimental import pallas as pl
from jax.experimental.pallas import tpu as pltpu

F32 = jnp.float32
BF16 = jnp.bfloat16

EPS = 1e-6
NEG = -1e30
CHUNK = 128
SGU_GROUPS = 8
HEAD_DIM = 64
N_HEADS_B = 8
WINDOWS = (128, 512, 2048)
DILATIONS = (1, 4, 16)
SPAN = 128
PEER_HEADS = 8
PEER_NKEYS = 128
PEER_TOPK = 16

LANES = 128
VMEM_LIMIT = 56 * 1024 * 1024


def _cparams(*sem):
    return pltpu.CompilerParams(dimension_semantics=sem, vmem_limit_bytes=VMEM_LIMIT)


def _rms(x, g):
    return x * lax.rsqrt(jnp.mean(x * x, axis=-1, keepdims=True) + EPS) * g


def _gelu(x):
    return 0.5 * x * (1.0 + lax.erf(x * (1.0 / math.sqrt(2.0))))


def _dot(a, b):
    return jnp.dot(a, b, preferred_element_type=F32)


def _dot_nt(a, b):
    return lax.dot_general(a, b, (((1,), (1,)), ((), ())), preferred_element_type=F32)


def _full(shape):
    nd = len(shape)
    return pl.BlockSpec(shape, lambda *_: (0,) * nd)


def _sgu_front(x, gmix_ref, win_ref, gv_ref):
    d = x.shape[-1]
    xn = _rms(x, gmix_ref[...]).astype(BF16)
    uv = _gelu(_dot(xn, win_ref[...]))
    return uv[:, :d], _rms(uv[:, d:], gv_ref[...])


def _sgu_seq_kernel(x_ref, gmix_ref, win_ref, gv_ref, ws_ref, bs_ref, wout_ref, h_ref, mixed_ref):
    x = x_ref[...]
    u, v = _sgu_front(x, gmix_ref, win_ref, gv_ref)
    vb = v.astype(BF16)
    gw = v.shape[-1] // SGU_GROUPS
    row = lax.broadcasted_iota(jnp.int32, (CHUNK, CHUNK), 0)
    col = lax.broadcasted_iota(jnp.int32, (CHUNK, CHUNK), 1)
    for g in range(SGU_GROUPS):
        wsg = jnp.where(col <= row, ws_ref[g], 0.0).astype(BF16)
        for c in range(x.shape[0] // CHUNK):
            rs = slice(c * CHUNK, (c + 1) * CHUNK)
            cs = slice(g * gw, (g + 1) * gw)
            mixed_ref[rs, cs] = _dot(wsg, vb[rs, cs]) + bs_ref[:, cs]
    z = (u * mixed_ref[...]).astype(BF16)
    h_ref[...] = x + _dot(z, wout_ref[...])


def _sgu_first_kernel(x_ref, gmix_ref, win_ref, gv_ref, ws0_ref, bs0_ref, wout_ref, h_ref, v_ref):
    x = x_ref[...]
    u, v = _sgu_front(x, gmix_ref, win_ref, gv_ref)
    v_ref[...] = v
    z = (u * (v * ws0_ref[...] + bs0_ref[...])).astype(BF16)
    h_ref[...] = x + _dot(z, wout_ref[...])


def _sgu_seq(x, gmix, win, gv, ws, bs_full, wout, tb):
    n, d = x.shape
    return pl.pallas_call(
        _sgu_seq_kernel,
        grid=(n // tb,),
        in_specs=[pl.BlockSpec((tb, d), lambda i: (i, 0)), _full(gmix.shape), _full(win.shape),
                  _full(gv.shape), _full(ws.shape), _full(bs_full.shape), _full(wout.shape)],
        out_specs=pl.BlockSpec((tb, d), lambda i: (i, 0)),
        out_shape=jax.ShapeDtypeStruct((n, d), F32),
        scratch_shapes=[pltpu.VMEM((tb, d), F32)],
        compiler_params=_cparams("parallel"),
        name="sgu_seq",
    )(x, gmix, win, gv, ws, bs_full, wout)


def _sgu_first(x, gmix, win, gv, ws0, bs0, wout):
    n, d = x.shape
    return pl.pallas_call(
        _sgu_first_kernel,
        grid=(1,),
        in_specs=[_full(x.shape), _full(gmix.shape), _full(win.shape), _full(gv.shape),
                  _full(ws0.shape), _full(bs0.shape), _full(wout.shape)],
        out_specs=[_full((n, d)), _full((n, d))],
        out_shape=[jax.ShapeDtypeStruct((n, d), F32), jax.ShapeDtypeStruct((n, d), F32)],
        compiler_params=_cparams("arbitrary"),
        name="sgu_first",
    )(x, gmix, win, gv, ws0, bs0, wout)


def _top_values(work, n_rows, out_ref, out_idx):
    iota = lax.broadcasted_iota(jnp.int32, work.shape, 0).astype(F32)
    for r in range(PEER_TOPK):
        m = jnp.max(work, axis=0, keepdims=True)
        first = jnp.min(jnp.where(work == m, iota, float(n_rows)), axis=0, keepdims=True)
        work = jnp.where(iota == first, -jnp.inf, work)
        out_ref[out_idx, r:r + 1, :] = m


def _peer_select_kernel(h_ref, g_ref, wq_ref, sk_ref, xn_ref, sc_ref, e_ref, tau_ref,
                        q_scr, tv_scr, tc_scr):
    xn = _rms(h_ref[...], g_ref[...]).astype(BF16)
    xn_ref[...] = xn
    q_scr[...] = _dot(xn, wq_ref[...])
    dk = sk_ref.shape[-1]

    def half_body(hc, carry):
        qc = q_scr[:, pl.ds(pl.multiple_of(hc * dk, dk), dk)].astype(BF16)
        sc = _dot_nt(sk_ref[hc], qc)
        sc_ref[hc] = sc
        _top_values(sc, PEER_NKEYS, tv_scr, hc)
        return carry

    lax.fori_loop(0, 2 * PEER_HEADS, half_body, 0)

    def head_body(h, carry):
        t1 = tv_scr[2 * h]
        t2 = tv_scr[2 * h + 1]
        cand = jnp.concatenate([t1[i:i + 1, :] + t2 for i in range(PEER_TOPK)], axis=0)
        _top_values(cand, PEER_TOPK * PEER_TOPK, tc_scr, 0)
        top = tc_scr[0]
        tau_ref[h] = top[PEER_TOPK - 1:PEER_TOPK, :]
        z = jnp.sum(jnp.exp(top - top[0:1, :]), axis=0, keepdims=True)
        e_ref[2 * h] = jnp.exp(sc_ref[2 * h] - t1[0:1, :]) / z
        e_ref[2 * h + 1] = jnp.exp(sc_ref[2 * h + 1] - t2[0:1, :])
        return carry

    lax.fori_loop(0, PEER_HEADS, head_body, 0)


def _peer_select(h, g, wq, sk, tb):
    n, d = h.shape
    nhc, nk, dk = sk.shape
    return pl.pallas_call(
        _peer_select_kernel,
        grid=(n // tb,),
        in_specs=[pl.BlockSpec((tb, d), lambda i: (i, 0)), _full(g.shape), _full(wq.shape), _full(sk.shape)],
        out_specs=[pl.BlockSpec((tb, d), lambda i: (i, 0)),
                   pl.BlockSpec((nhc, nk, tb), lambda i: (0, 0, i)),
                   pl.BlockSpec((nhc, nk, tb), lambda i: (0, 0, i)),
                   pl.BlockSpec((PEER_HEADS, 1, tb), lambda i: (0, 0, i))],
        out_shape=[jax.ShapeDtypeStruct((n, d), BF16),
                   jax.ShapeDtypeStruct((nhc, nk, n), F32),
                   jax.ShapeDtypeStruct((nhc, nk, n), F32),
                   jax.ShapeDtypeStruct((PEER_HEADS, 1, n), F32)],
        scratch_shapes=[pltpu.VMEM((tb, wq.shape[1]), F32),
                        pltpu.VMEM((nhc, PEER_TOPK, tb), F32),
                        pltpu.VMEM((1, PEER_TOPK, tb), F32)],
        compiler_params=_cparams("parallel"),
        name="peer_select",
    )(h, g, wq, sk)


def _peer_dense_kernel(xn_ref, u_ref, vt_ref, sc_ref, e_ref, tau_ref, h_ref, o_ref, acc_ref, w_scr):
    j = pl.program_id(1)
    eb = u_ref.shape[0]
    rows_per_block = eb // PEER_NKEYS

    @pl.when(j == 0)
    def _():
        acc_ref[...] = jnp.zeros_like(acc_ref)

    act = _dot_nt(u_ref[...], xn_ref[...])
    for al in range(rows_per_block):
        a = j * rows_per_block + al
        gate = jnp.zeros((PEER_NKEYS, act.shape[1]), F32)
        for h in range(PEER_HEADS):
            s1 = sc_ref[2 * h, pl.ds(a, 1), :]
            e1 = e_ref[2 * h, pl.ds(a, 1), :]
            s = sc_ref[2 * h + 1] + s1
            gate = gate + jnp.where(s >= tau_ref[h], e_ref[2 * h + 1] * e1, 0.0)
        rs = slice(al * PEER_NKEYS, (al + 1) * PEER_NKEYS)
        w_scr[rs, :] = (gate * _gelu(act[rs, :])).astype(BF16)
    acc_ref[...] += _dot(vt_ref[...], w_scr[...])

    @pl.when(j == pl.num_programs(1) - 1)
    def _():
        o_ref[...] = h_ref[...] + acc_ref[...].T


def _peer_dense(xn, u, vt, sc, e, tau, h, tb, eb):
    n, d = h.shape
    n_exp = u.shape[0]
    nhc, nk, _ = sc.shape
    return pl.pallas_call(
        _peer_dense_kernel,
        grid=(n // tb, n_exp // eb),
        in_specs=[pl.BlockSpec((tb, d), lambda i, j: (i, 0)),
                  pl.BlockSpec((eb, d), lambda i, j: (j, 0)),
                  pl.BlockSpec((d, eb), lambda i, j: (0, j)),
                  pl.BlockSpec((nhc, nk, tb), lambda i, j: (0, 0, i)),
                  pl.BlockSpec((nhc, nk, tb), lambda i, j: (0, 0, i)),
                  pl.BlockSpec((PEER_HEADS, 1, tb), lambda i, j: (0, 0, i)),
                  pl.BlockSpec((tb, d), lambda i, j: (i, 0))],
        out_specs=pl.BlockSpec((tb, d), lambda i, j: (i, 0)),
        out_shape=jax.ShapeDtypeStruct((n, d), F32),
        scratch_shapes=[pltpu.VMEM((d, tb), F32), pltpu.VMEM((eb, tb), BF16)],
        compiler_params=_cparams("parallel", "arbitrary"),
        name="peer_dense",
    )(xn, u, vt, sc, e, tau, h)


def _peer(h, g, wq, sk, u, vt, tb_sel, tb, eb):
    xn, sc, e, tau = _peer_select(h, g, wq, sk, tb_sel)
    return _peer_dense(xn, u, vt, sc, e, tau, h, tb, eb)


def _ple_kernel(h_ref, p_ref, g_ref, wg_ref, wp_ref, gf_ref, o_ref, *, final):
    h = h_ref[...]
    gate = jax.nn.sigmoid(_dot(_rms(h, g_ref[...]).astype(BF16), wg_ref[...]))
    h = h + gate * _dot(p_ref[...].astype(BF16), wp_ref[...])
    o_ref[...] = _rms(h, gf_ref[...]) if final else h


def _ple(h, p, g, wg, wp, gf, tb, final):
    n, d = h.shape
    return pl.pallas_call(
        functools.partial(_ple_kernel, final=final),
        grid=(n // tb,),
        in_specs=[pl.BlockSpec((tb, d), lambda i: (i, 0)), pl.BlockSpec((tb, p.shape[1]), lambda i: (i, 0)),
                  _full(g.shape), _full(wg.shape), _full(wp.shape), _full(gf.shape)],
        out_specs=pl.BlockSpec((tb, d), lambda i: (i, 0)),
        out_shape=jax.ShapeDtypeStruct((n, d), F32),
        compiler_params=_cparams("parallel"),
        name="ple_final" if final else "ple",
    )(h, p, g, wg, wp, gf)


def _qkv_kernel(h_ref, gm_ref, gkv_ref, wq_ref, wkv_ref, q_ref, kv_ref):
    h = h_ref[...]
    q_ref[...] = _dot(_rms(h, gm_ref[...]).astype(BF16), wq_ref[...]).astype(BF16)
    kv_ref[...] = _dot(_rms(h, gkv_ref[...]).astype(BF16), wkv_ref[...])


def _qkv(h, gm, gkv, wq, wkv, tb):
    n, d = h.shape
    return pl.pallas_call(
        _qkv_kernel,
        grid=(n // tb,),
        in_specs=[pl.BlockSpec((tb, d), lambda i: (i, 0)), _full(gm.shape), _full(gkv.shape),
                  _full(wq.shape), _full(wkv.shape)],
        out_specs=[pl.BlockSpec((tb, wq.shape[1]), lambda i: (i, 0)),
                   pl.BlockSpec((tb, wkv.shape[1]), lambda i: (i, 0))],
        out_shape=[jax.ShapeDtypeStruct((n, wq.shape[1]), BF16),
                   jax.ShapeDtypeStruct((n, wkv.shape[1]), F32)],
        compiler_params=_cparams("parallel"),
        name="qkv",
    )(h, gm, gkv, wq, wkv)


def _alibi_slopes():
    n = len(WINDOWS) * N_HEADS_B
    e = (np.arange(n, dtype=np.float32) + 1.0) * np.float32(8.0 / n)
    return np.exp2(-e).astype(np.float32).reshape(len(WINDOWS), N_HEADS_B)


def _attn_seq_kernel(q_ref, kc_ref, kp_ref, vc_ref, vp_ref, o_ref, lse_ref, *, dilation, slopes):
    b = pl.program_id(1)
    qi = lax.broadcasted_iota(jnp.int32, (SPAN, SPAN), 0)
    ki = lax.broadcasted_iota(jnp.int32, (SPAN, SPAN), 1)
    dist_c = (qi - ki).astype(F32) * float(dilation)
    dist_p = (qi - ki + SPAN).astype(F32) * float(dilation)
    valid_c = ki <= qi
    valid_p = jnp.logical_and(ki >= qi, b > 0)
    scale = HEAD_DIM ** -0.5
    for h in range(N_HEADS_B):
        hs = slice(h * HEAD_DIM, (h + 1) * HEAD_DIM)
        qh = q_ref[:, hs]
        s_c = _dot_nt(qh, kc_ref[:, hs].astype(BF16)) * scale - slopes[h] * dist_c
        s_p = _dot_nt(qh, kp_ref[:, hs].astype(BF16)) * scale - slopes[h] * dist_p
        s_c = jnp.where(valid_c, s_c, NEG)
        s_p = jnp.where(valid_p, s_p, NEG)
        m = jnp.maximum(jnp.max(s_c, axis=-1, keepdims=True), jnp.max(s_p, axis=-1, keepdims=True))
        e_c = jnp.exp(s_c - m)
        e_p = jnp.exp(s_p - m)
        l = jnp.sum(e_c, axis=-1, keepdims=True) + jnp.sum(e_p, axis=-1, keepdims=True)
        o = _dot(e_c.astype(BF16), vc_ref[:, hs].astype(BF16)) + _dot(e_p.astype(BF16), vp_ref[:, hs].astype(BF16))
        o_ref[:, hs] = o / l
        lse_ref[:, hs] = jnp.broadcast_to(m + jnp.log(l), (SPAN, HEAD_DIM))


def _attn_seq(q, kv, branch):
    s, qw = q.shape
    r = DILATIONS[branch]
    hw = N_HEADS_B * HEAD_DIM
    nq = qw // hw
    nkv = kv.shape[1] // hw
    qr = q.reshape(s // r, r * qw)
    kvr = kv.reshape(s // r, r * kv.shape[1])
    kcol = 2 * branch
    blk = (SPAN, hw)
    prev = lambda c, b: jnp.maximum(b - 1, 0)
    o, lse = pl.pallas_call(
        functools.partial(_attn_seq_kernel, dilation=r, slopes=[float(v) for v in _alibi_slopes()[branch]]),
        grid=(r, s // (r * SPAN)),
        in_specs=[pl.BlockSpec(blk, lambda c, b: (b, c * nq + branch)),
                  pl.BlockSpec(blk, lambda c, b: (b, c * nkv + kcol)),
                  pl.BlockSpec(blk, lambda c, b: (prev(c, b), c * nkv + kcol)),
                  pl.BlockSpec(blk, lambda c, b: (b, c * nkv + kcol + 1)),
                  pl.BlockSpec(blk, lambda c, b: (prev(c, b), c * nkv + kcol + 1))],
        out_specs=[pl.BlockSpec(blk, lambda c, b: (b, c)), pl.BlockSpec(blk, lambda c, b: (b, c))],
        out_shape=[jax.ShapeDtypeStruct((s // r, r * hw), F32), jax.ShapeDtypeStruct((s // r, r * hw), F32)],
        compiler_params=_cparams("parallel", "arbitrary"),
        name=f"attn_seq_w{WINDOWS[branch]}",
    )(qr, kvr, kvr, kvr, kvr)
    return o.reshape(s, hw), lse.reshape(s, hw)


def _attn_step_kernel(q_ref, kvn_ref, ind_ref, indt_ref, sl_ref, *refs):
    n_br = len(WINDOWS)
    state_refs, (o_ref, lse_ref) = refs[:2 * n_br], refs[2 * n_br:]
    hw = N_HEADS_B * HEAD_DIM
    ind = ind_ref[...]
    indt = indt_ref[...]
    scale = HEAD_DIM ** -0.5

    def split_dot(a, b):
        hi = a.astype(BF16)
        lo = (a - hi.astype(F32)).astype(BF16)
        return _dot(hi, b) + _dot(lo, b)

    back = (SPAN - lax.broadcasted_iota(jnp.int32, (SPAN, LANES), 0)).astype(F32)
    for g in range(n_br):
        q = q_ref[0, :, g * hw:(g + 1) * hw].astype(F32)
        k_new = kvn_ref[0, :, 2 * g * hw:(2 * g + 1) * hw]
        v_new = kvn_ref[0, :, (2 * g + 1) * hw:(2 * g + 2) * hw]
        k_old = state_refs[2 * g][0]
        v_old = state_refs[2 * g + 1][0]
        slope = sl_ref[g:g + 1, :]
        s_old = split_dot(k_old * q, ind) * scale - slope * (back * float(DILATIONS[g]))
        s_new = split_dot(jnp.broadcast_to(k_new * q, (8, hw)), ind)[0:1, :] * scale
        m = jnp.maximum(jnp.max(s_old, axis=0, keepdims=True), s_new)
        e_old = jnp.exp(s_old - m)
        e_new = jnp.exp(s_new - m)
        l = jnp.sum(e_old, axis=0, keepdims=True) + e_new
        ex_old = split_dot(e_old, indt)
        tail = jnp.concatenate([e_new, l, m + jnp.log(l), jnp.zeros((5, LANES), F32)], axis=0)
        tail = split_dot(tail, indt)
        o = jnp.sum(ex_old * v_old, axis=0, keepdims=True) + tail[0:1, :] * v_new
        o_ref[0, :, g * hw:(g + 1) * hw] = o / tail[1:2, :]
        lse_ref[0, :, g * hw:(g + 1) * hw] = tail[2:3, :]


def _attn_step(q, kv_new, states):
    nb = states[0].shape[0]
    hw = N_HEADS_B * HEAD_DIM
    n_br = len(WINDOWS)
    ind = np.zeros((hw, LANES), np.float32)
    ind[:, :N_HEADS_B] = np.kron(np.eye(N_HEADS_B, dtype=np.float32), np.ones((HEAD_DIM, 1), np.float32))
    slopes = np.zeros((n_br, LANES), np.float32)
    slopes[:, :N_HEADS_B] = _alibi_slopes()
    slopes = jnp.asarray(slopes)
    q3 = q[:nb].reshape(nb, 1, q.shape[1])
    kv3 = kv_new[:nb].reshape(nb, 1, kv_new.shape[1])
    st_args, st_specs = [], []
    for g in range(n_br):
        r = DILATIONS[g]
        lb = states[g].shape[1]
        assert lb == WINDOWS[g], "window buffers must be full"
        st = states[g].reshape(nb, SPAN, r * 2 * hw)
        st_args += [st, st]
        st_specs += [pl.BlockSpec((1, SPAN, hw), lambda n: (n, 0, 0)), pl.BlockSpec((1, SPAN, hw), lambda n: (n, 0, 1))]
    o, lse = pl.pallas_call(
        _attn_step_kernel,
        grid=(nb,),
        in_specs=[pl.BlockSpec((1, 1, q.shape[1]), lambda n: (n, 0, 0)),
                  pl.BlockSpec((1, 1, kv_new.shape[1]), lambda n: (n, 0, 0)),
                  _full(ind.shape), _full(ind.T.shape), _full(slopes.shape)] + st_specs,
        out_specs=[pl.BlockSpec((1, 1, n_br * hw), lambda n: (n, 0, 0)),
                   pl.BlockSpec((1, 1, n_br * hw), lambda n: (n, 0, 0))],
        out_shape=[jax.ShapeDtypeStruct((nb, 1, n_br * hw), F32), jax.ShapeDtypeStruct((nb, 1, n_br * hw), F32)],
        compiler_params=_cparams("parallel"),
        name="attn_step",
    )(q3, kv3, jnp.asarray(ind, BF16), jnp.asarray(ind.T, BF16), slopes, *st_args)
    return o.reshape(nb, n_br * hw), lse.reshape(nb, n_br * hw)


def _attn_out_kernel(*refs):
    n_br = len(WINDOWS)
    o_refs, lse_refs = refs[:n_br], refs[n_br:2 * n_br]
    h_ref, wo_ref, out_ref = refs[2 * n_br:]
    lses = [r[...] for r in lse_refs]
    m = functools.reduce(jnp.maximum, lses)
    ws = [jnp.exp(l - m) for l in lses]
    tot = functools.reduce(lambda a, b: a + b, ws)
    o = functools.reduce(lambda a, b: a + b, [w / tot * r[...] for w, r in zip(ws, o_refs)])
    out_ref[...] = h_ref[...] + _dot(o.astype(BF16), wo_ref[...])


def _attn_out(os, lses, h, wo, tb):
    n, d = h.shape
    hw = wo.shape[0]
    bs = pl.BlockSpec((tb, hw), lambda i: (i, 0))
    return pl.pallas_call(
        _attn_out_kernel,
        grid=(n // tb,),
        in_specs=[bs] * (2 * len(os)) + [pl.BlockSpec((tb, d), lambda i: (i, 0)), _full(wo.shape)],
        out_specs=pl.BlockSpec((tb, d), lambda i: (i, 0)),
        out_shape=jax.ShapeDtypeStruct((n, d), F32),
        compiler_params=_cparams("parallel"),
        name="attn_out",
    )(*os, *lses, h, wo)


def kernel(x_prompt, x_sample, state_kv_w128, state_kv_w512, state_kv_w2048, p_prompt, p_sample, g_mix, sgu_w_in, sgu_g_v, sgu_w_s, sgu_b_s, sgu_w_out, kv_g, w_kv, attn_w_q, attn_w_o, g_ffn, peer_w_query, peer_subkeys, peer_u, peer_v, ple_g, ple_w_gate, ple_w_proj, g_final):
    depth, d = g_mix.shape
    assert depth == 2 and sgu_w_in.shape[0] == 1 and attn_w_q.shape[0] == 1
    b_p, s_p, _ = x_prompt.shape
    b_s, t_s, _ = x_sample.shape
    assert b_p == 1 and t_s == 1 and s_p % max(WINDOWS) == 0
    states = (state_kv_w128, state_kv_w512, state_kv_w2048)
    hw = N_HEADS_B * HEAD_DIM
    n_br = len(WINDOWS)

    row = lambda a: a.reshape(1, -1).astype(F32)
    bf = lambda a: a.astype(BF16)

    w_in, w_out = bf(sgu_w_in[0]), bf(sgu_w_out[0])
    gw = d // SGU_GROUPS
    bs_full = jnp.repeat(sgu_b_s[0].T, gw, axis=1)
    ws0 = jnp.repeat(sgu_w_s[0][:, 0, 0], gw).reshape(1, d)
    bs0 = bs_full[0:1]
    wq_peer = bf(peer_w_query)
    subkeys = bf(peer_subkeys.reshape(depth, 2 * PEER_HEADS, PEER_NKEYS, -1))
    u_tab = bf(peer_u)
    vt_tab = bf(jnp.swapaxes(peer_v, 1, 2))
    w_gate, w_proj = bf(ple_w_gate), bf(ple_w_proj)
    w_q, w_kvb, w_o = bf(attn_w_q[0]), bf(w_kv), bf(attn_w_o[0])

    n_s = LANES
    xs = jnp.pad(x_sample.reshape(b_s, d), ((0, n_s - b_s), (0, 0)))
    ps = jnp.pad(p_sample.reshape(depth, b_s, -1), ((0, 0), (0, n_s - b_s), (0, 0)))
    xp = x_prompt.reshape(s_p, d)
    pp = p_prompt.reshape(depth, s_p, -1)

    def channel(h, p, i, tb_sel, tb, tbp, final):
        h = _peer(h, row(g_ffn[i]), wq_peer[i], subkeys[i], u_tab[i], vt_tab[i], tb_sel, tb, 1024)
        return _ple(h, p[i], row(ple_g[i]), w_gate[i], w_proj[i], row(g_final), tbp, final)

    hp = _sgu_seq(xp, row(g_mix[0]), w_in, row(sgu_g_v[0]), sgu_w_s[0], bs_full, w_out, 512)
    hs, v_s = _sgu_first(xs, row(g_mix[0]), w_in, row(sgu_g_v[0]), ws0, bs0, w_out)
    hp = channel(hp, pp, 0, 256, 512, 512, False)
    hs = channel(hs, ps, 0, n_s, n_s, n_s, False)

    qp, kvp = _qkv(hp, row(g_mix[1]), row(kv_g), w_q, w_kvb, 512)
    qs, kvs = _qkv(hs, row(g_mix[1]), row(kv_g), w_q, w_kvb, n_s)
    outs = [_attn_seq(qp, kvp, g) for g in range(n_br)]
    hp = _attn_out([o for o, _ in outs], [l for _, l in outs], hp, w_o, 512)
    o_s, lse_s = _attn_step(qs, kvs, states)
    pad = lambda a: jnp.pad(a, ((0, n_s - b_s), (0, 0)))
    hs = _attn_out([pad(o_s[:, g * hw:(g + 1) * hw]) for g in range(n_br)],
                   [pad(lse_s[:, g * hw:(g + 1) * hw]) for g in range(n_br)], hs, w_o, n_s)
    yp = channel(hp, pp, 1, 256, 512, 512, True)
    ys = channel(hs, ps, 1, n_s, n_s, n_s, True)

    kv_p = kvp.reshape(1, s_p, n_br, 2, N_HEADS_B, HEAD_DIM)
    kv_n = kvs[:b_s].reshape(b_s, 1, n_br, 2, N_HEADS_B, HEAD_DIM)
    new_p = [kv_p[:, s_p - min(WINDOWS[g], s_p):, g] for g in range(n_br)]
    new_s = []
    for g in range(n_br):
        full_len = states[g].shape[1] + 1
        keep = min(WINDOWS[g], full_len)
        new_s.append(jnp.concatenate([states[g][:, full_len - keep:], kv_n[:, :, g]], axis=1))
    return (yp.reshape(1, s_p, d), ys[:b_s].reshape(b_s, 1, d), v_s[:b_s].reshape(1, b_s, 1, d),
            new_p[0], new_p[1], new_p[2], new_s[0], new_s[1], new_s[2])
```

```python
import functools
import math

import numpy as np
import jax
import jax.numpy as jnp
from jax import lax
from jax.experimental import pallas as pl
from jax.experimental.pallas import tpu as pltpu

F32 = jnp.float32
BF16 = jnp.bfloat16

EPS = 1e-6
NEG = -1e30
CHUNK = 128
SGU_GROUPS = 8
HEAD_DIM = 64
N_HEADS_B = 8
WINDOWS = (128, 512, 2048)
DILATIONS = (1, 4, 16)
SPAN = 128
PEER_HEADS = 8
PEER_NKEYS = 128
PEER_TOPK = 16

LANES = 128
VMEM_LIMIT = 56 * 1024 * 1024


def _cparams(*sem):
    return pltpu.CompilerParams(dimension_semantics=sem, vmem_limit_bytes=VMEM_LIMIT)


def _rms(x, g):
    return x * lax.rsqrt(jnp.mean(x * x, axis=-1, keepdims=True) + EPS) * g


def _gelu(x):
    return 0.5 * x * (1.0 + lax.erf(x * (1.0 / math.sqrt(2.0))))


def _dot(a, b):
    return jnp.dot(a, b, preferred_element_type=F32)


def _dot_nt(a, b):
    return lax.dot_general(a, b, (((1,), (1,)), ((), ())), preferred_element_type=F32)


def _full(shape):
    nd = len(shape)
    return pl.BlockSpec(shape, lambda *_: (0,) * nd)


def _sgu_front(x, gmix_ref, win_ref, gv_ref):
    d = x.shape[-1]
    xn = _rms(x, gmix_ref[...]).astype(BF16)
    uv = _gelu(_dot(xn, win_ref[...]))
    return uv[:, :d], _rms(uv[:, d:], gv_ref[...])


def _sgu_seq_kernel(x_ref, gmix_ref, win_ref, gv_ref, ws_ref, bs_ref, wout_ref, h_ref, mixed_ref):
    x = x_ref[...]
    u, v = _sgu_front(x, gmix_ref, win_ref, gv_ref)
    vb = v.astype(BF16)
    gw = v.shape[-1] // SGU_GROUPS
    row = lax.broadcasted_iota(jnp.int32, (CHUNK, CHUNK), 0)
    col = lax.broadcasted_iota(jnp.int32, (CHUNK, CHUNK), 1)
    for g in range(SGU_GROUPS):
        wsg = jnp.where(col <= row, ws_ref[g], 0.0).astype(BF16)
        for c in range(x.shape[0] // CHUNK):
            rs = slice(c * CHUNK, (c + 1) * CHUNK)
            cs = slice(g * gw, (g + 1) * gw)
            mixed_ref[rs, cs] = _dot(wsg, vb[rs, cs]) + bs_ref[:, cs]
    z = (u * mixed_ref[...]).astype(BF16)
    h_ref[...] = x + _dot(z, wout_ref[...])


def _sgu_first_kernel(x_ref, gmix_ref, win_ref, gv_ref, ws0_ref, bs0_ref, wout_ref, h_ref, v_ref):
    x = x_ref[...]
    u, v = _sgu_front(x, gmix_ref, win_ref, gv_ref)
    v_ref[...] = v
    z = (u * (v * ws0_ref[...] + bs0_ref[...])).astype(BF16)
    h_ref[...] = x + _dot(z, wout_ref[...])


def _sgu_seq(x, gmix, win, gv, ws, bs_full, wout, tb):
    n, d = x.shape
    return pl.pallas_call(
        _sgu_seq_kernel,
        grid=(n // tb,),
        in_specs=[pl.BlockSpec((tb, d), lambda i: (i, 0)), _full(gmix.shape), _full(win.shape),
                  _full(gv.shape), _full(ws.shape), _full(bs_full.shape), _full(wout.shape)],
        out_specs=pl.BlockSpec((tb, d), lambda i: (i, 0)),
        out_shape=jax.ShapeDtypeStruct((n, d), F32),
        scratch_shapes=[pltpu.VMEM((tb, d), F32)],
        compiler_params=_cparams("parallel"),
        name="sgu_seq",
    )(x, gmix, win, gv, ws, bs_full, wout)


def _sgu_first(x, gmix, win, gv, ws0, bs0, wout):
    n, d = x.shape
    return pl.pallas_call(
        _sgu_first_kernel,
        grid=(1,),
        in_specs=[_full(x.shape), _full(gmix.shape), _full(win.shape), _full(gv.shape),
                  _full(ws0.shape), _full(bs0.shape), _full(wout.shape)],
        out_specs=[_full((n, d)), _full((n, d))],
        out_shape=[jax.ShapeDtypeStruct((n, d), F32), jax.ShapeDtypeStruct((n, d), F32)],
        compiler_params=_cparams("arbitrary"),
        name="sgu_first",
    )(x, gmix, win, gv, ws0, bs0, wout)


_CAND_COUNTS = [PEER_TOPK // (i + 1) for i in range(PEER_TOPK)]
_N_CAND = sum(_CAND_COUNTS)
_N_CAND_PAD = -(-_N_CAND // 8) * 8
TOKEN_GROUP = 16
TILE_ROW_STRIDE = 24


def _extract_top(work, payloads, emit):
    iota = lax.broadcasted_iota(jnp.int32, work.shape, 0).astype(F32)
    n_rows = float(work.shape[0])
    for r in range(PEER_TOPK):
        m = jnp.max(work, axis=0, keepdims=True)
        first = jnp.min(jnp.where(work == m, iota, n_rows), axis=0, keepdims=True)
        hit = iota == first
        emit(r, m, first, [jnp.sum(jnp.where(hit, p, 0.0), axis=0, keepdims=True) for p in payloads])
        work = jnp.where(hit, -jnp.inf, work)


def _peer_select_kernel(h_ref, g_ref, wq_ref, sk_ref, xn_ref, i1_ref, i2_ref, gate_ref,
                        q_scr, tv_scr, ti_scr, top_scr, a_scr, b_scr, g_scr):
    tb = h_ref.shape[0]
    xn = _rms(h_ref[...], g_ref[...]).astype(BF16)
    xn_ref[...] = xn
    q_scr[...] = _dot(xn, wq_ref[...])
    dk = sk_ref.shape[-1]

    def head_body(h, carry):
        for hc in (2 * h, 2 * h + 1):
            qc = q_scr[:, pl.ds(pl.multiple_of(hc * dk, dk), dk)].astype(BF16)
            sc = _dot_nt(sk_ref[hc], qc)

            def emit(r, m, first, _, hc=hc):
                tv_scr[hc, r:r + 1, :] = m
                ti_scr[hc, r:r + 1, :] = first

            _extract_top(sc, [], emit)

        t1, t2 = tv_scr[2 * h], tv_scr[2 * h + 1]
        k1, k2 = ti_scr[2 * h], ti_scr[2 * h + 1]
        pad = _N_CAND_PAD - _N_CAND
        cand = jnp.concatenate([t1[i:i + 1, :] + t2[:nj, :] for i, nj in enumerate(_CAND_COUNTS)]
                               + [jnp.full((pad, tb), -jnp.inf, F32)], axis=0)
        ka = jnp.concatenate([jnp.broadcast_to(k1[i:i + 1, :], (nj, tb)) for i, nj in enumerate(_CAND_COUNTS)]
                             + [jnp.zeros((pad, tb), F32)], axis=0)
        kb = jnp.concatenate([k2[:nj, :] for nj in _CAND_COUNTS] + [jnp.zeros((pad, tb), F32)], axis=0)

        def emit(r, m, first, picked):
            top_scr[r:r + 1, :] = m
            a_scr[h, r:r + 1, :] = picked[0]
            b_scr[h, r:r + 1, :] = picked[1]

        _extract_top(cand, [ka, kb], emit)
        e = jnp.exp(top_scr[...] - top_scr[0:1, :])
        g_scr[h] = e / jnp.sum(e, axis=0, keepdims=True)
        return carry

    lax.fori_loop(0, PEER_HEADS, head_body, 0)
    n_sel = PEER_HEADS * PEER_TOPK
    i1_ref[...] = a_scr[...].reshape(n_sel, tb).T
    i2_ref[...] = b_scr[...].reshape(n_sel, tb).T
    gate_ref[...] = g_scr[...].reshape(n_sel, tb).T


def _peer_select(h, g, wq, sk, tb):
    n, d = h.shape
    nhc, nk, dk = sk.shape
    n_sel = PEER_HEADS * PEER_TOPK
    sel = pl.BlockSpec((tb, n_sel), lambda i: (i, 0))
    return pl.pallas_call(
        _peer_select_kernel,
        grid=(n // tb,),
        in_specs=[pl.BlockSpec((tb, d), lambda i: (i, 0)), _full(g.shape), _full(wq.shape), _full(sk.shape)],
        out_specs=[pl.BlockSpec((tb, d), lambda i: (i, 0)), sel, sel, sel],
        out_shape=[jax.ShapeDtypeStruct((n, d), BF16)] + [jax.ShapeDtypeStruct((n, n_sel), F32)] * 3,
        scratch_shapes=[pltpu.VMEM((tb, wq.shape[1]), F32),
                        pltpu.VMEM((nhc, PEER_TOPK, tb), F32),
                        pltpu.VMEM((nhc, PEER_TOPK, tb), F32),
                        pltpu.VMEM((PEER_TOPK, tb), F32),
                        pltpu.VMEM((PEER_HEADS, PEER_TOPK, tb), F32),
                        pltpu.VMEM((PEER_HEADS, PEER_TOPK, tb), F32),
                        pltpu.VMEM((PEER_HEADS, PEER_TOPK, tb), F32)],
        compiler_params=_cparams("parallel"),
        name="peer_select",
    )(h, g, wq, sk)


def _build_gates(i1_ref, i2_ref, g_ref, gates_ref, tile_ref):
    tb = i1_ref.shape[0]
    nk = PEER_NKEYS
    sub = lax.broadcasted_iota(jnp.int32, (nk, i1_ref.shape[1]), 0).astype(F32)

    def group_body(gi, carry):
        t0 = pl.multiple_of(gi * TOKEN_GROUP, TOKEN_GROUP)
        for tt in range(TOKEN_GROUP):
            row = pl.ds(t0 + tt, 1)
            first = jnp.where(sub == i1_ref[row, :], g_ref[row, :], 0.0).astype(BF16)
            second = jnp.where(sub == i2_ref[row, :], 1.0, 0.0).astype(BF16)
            tile_ref[pl.ds(tt, nk, stride=TILE_ROW_STRIDE), :] = _dot_nt(first, second)
        for a in range(nk):
            rows = tile_ref[a * TILE_ROW_STRIDE:a * TILE_ROW_STRIDE + TOKEN_GROUP, :]
            gates_ref[pl.ds(t0, TOKEN_GROUP), a * nk:(a + 1) * nk] = rows.astype(BF16)
        return carry

    lax.fori_loop(0, tb // TOKEN_GROUP, group_body, 0)


def _peer_dense_kernel(xn_ref, i1_ref, i2_ref, g_ref, u_ref, v_ref, h_ref, o_ref, acc_ref, gates_ref, tile_ref):
    j = pl.program_id(1)
    eb = u_ref.shape[0]

    @pl.when(j == 0)
    def _():
        acc_ref[...] = jnp.zeros_like(acc_ref)
        _build_gates(i1_ref, i2_ref, g_ref, gates_ref, tile_ref)

    act = _gelu(_dot_nt(xn_ref[...], u_ref[...]))
    gates = gates_ref[:, pl.ds(pl.multiple_of(j * eb, eb), eb)]
    acc_ref[...] += _dot((gates.astype(F32) * act).astype(BF16), v_ref[...])

    @pl.when(j == pl.num_programs(1) - 1)
    def _():
        o_ref[...] = h_ref[...] + acc_ref[...]


def _peer_dense(xn, i1, i2, gate, u, v, h, tb, eb):
    n, d = h.shape
    n_exp = u.shape[0]
    tok = lambda w: pl.BlockSpec((tb, w), lambda i, j: (i, 0))
    return pl.pallas_call(
        _peer_dense_kernel,
        grid=(n // tb, n_exp // eb),
        in_specs=[tok(d), tok(i1.shape[1]), tok(i1.shape[1]), tok(i1.shape[1]),
                  pl.BlockSpec((eb, d), lambda i, j: (j, 0)),
                  pl.BlockSpec((eb, d), lambda i, j: (j, 0)),
                  tok(d)],
        out_specs=tok(d),
        out_shape=jax.ShapeDtypeStruct((n, d), F32),
        scratch_shapes=[pltpu.VMEM((tb, d), F32), pltpu.VMEM((tb, n_exp), BF16),
                        pltpu.VMEM((PEER_NKEYS * TILE_ROW_STRIDE, PEER_NKEYS), F32)],
        compiler_params=_cparams("parallel", "arbitrary"),
        name="peer_dense",
    )(xn, i1, i2, gate, u, v, h)


def _peer(h, g, wq, sk, u, v, tb_sel, tb, eb):
    xn, i1, i2, gate = _peer_select(h, g, wq, sk, tb_sel)
    return _peer_dense(xn, i1, i2, gate, u, v, h, tb, eb)


def _ple_kernel(h_ref, p_ref, g_ref, wg_ref, wp_ref, gf_ref, o_ref, *, final):
    h = h_ref[...]
    gate = jax.nn.sigmoid(_dot(_rms(h, g_ref[...]).astype(BF16), wg_ref[...]))
    h = h + gate * _dot(p_ref[...].astype(BF16), wp_ref[...])
    o_ref[...] = _rms(h, gf_ref[...]) if final else h


def _ple(h, p, g, wg, wp, gf, tb, final):
    n, d = h.shape
    return pl.pallas_call(
        functools.partial(_ple_kernel, final=final),
        grid=(n // tb,),
        in_specs=[pl.BlockSpec((tb, d), lambda i: (i, 0)), pl.BlockSpec((tb, p.shape[1]), lambda i: (i, 0)),
                  _full(g.shape), _full(wg.shape), _full(wp.shape), _full(gf.shape)],
        out_specs=pl.BlockSpec((tb, d), lambda i: (i, 0)),
        out_shape=jax.ShapeDtypeStruct((n, d), F32),
        compiler_params=_cparams("parallel"),
        name="ple_final" if final else "ple",
    )(h, p, g, wg, wp, gf)


def _qkv_kernel(h_ref, gm_ref, gkv_ref, wq_ref, wkv_ref, q_ref, kv_ref):
    h = h_ref[...]
    q_ref[...] = _dot(_rms(h, gm_ref[...]).astype(BF16), wq_ref[...]).astype(BF16)
    kv_ref[...] = _dot(_rms(h, gkv_ref[...]).astype(BF16), wkv_ref[...])


def _qkv(h, gm, gkv, wq, wkv, tb):
    n, d = h.shape
    return pl.pallas_call(
        _qkv_kernel,
        grid=(n // tb,),
        in_specs=[pl.BlockSpec((tb, d), lambda i: (i, 0)), _full(gm.shape), _full(gkv.shape),
                  _full(wq.shape), _full(wkv.shape)],
        out_specs=[pl.BlockSpec((tb, wq.shape[1]), lambda i: (i, 0)),
                   pl.BlockSpec((tb, wkv.shape[1]), lambda i: (i, 0))],
        out_shape=[jax.ShapeDtypeStruct((n, wq.shape[1]), BF16),
                   jax.ShapeDtypeStruct((n, wkv.shape[1]), F32)],
        compiler_params=_cparams("parallel"),
        name="qkv",
    )(h, gm, gkv, wq, wkv)


def _alibi_slopes():
    n = len(WINDOWS) * N_HEADS_B
    e = (np.arange(n, dtype=np.float32) + 1.0) * np.float32(8.0 / n)
    return np.exp2(-e).astype(np.float32).reshape(len(WINDOWS), N_HEADS_B)


def _attn_seq_kernel(q_ref, kc_ref, kp_ref, vc_ref, vp_ref, o_ref, lse_ref, *, dilation, slopes):
    b = pl.program_id(1)
    qi = lax.broadcasted_iota(jnp.int32, (SPAN, SPAN), 0)
    ki = lax.broadcasted_iota(jnp.int32, (SPAN, SPAN), 1)
    dist_c = (qi - ki).astype(F32) * float(dilation)
    dist_p = (qi - ki + SPAN).astype(F32) * float(dilation)
    valid_c = ki <= qi
    valid_p = jnp.logical_and(ki >= qi, b > 0)
    scale = HEAD_DIM ** -0.5
    for h in range(N_HEADS_B):
        hs = slice(h * HEAD_DIM, (h + 1) * HEAD_DIM)
        qh = q_ref[:, hs]
        s_c = _dot_nt(qh, kc_ref[:, hs].astype(BF16)) * scale - slopes[h] * dist_c
        s_p = _dot_nt(qh, kp_ref[:, hs].astype(BF16)) * scale - slopes[h] * dist_p
        s_c = jnp.where(valid_c, s_c, NEG)
        s_p = jnp.where(valid_p, s_p, NEG)
        m = jnp.maximum(jnp.max(s_c, axis=-1, keepdims=True), jnp.max(s_p, axis=-1, keepdims=True))
        e_c = jnp.exp(s_c - m)
        e_p = jnp.exp(s_p - m)
        l = jnp.sum(e_c, axis=-1, keepdims=True) + jnp.sum(e_p, axis=-1, keepdims=True)
        o = _dot(e_c.astype(BF16), vc_ref[:, hs].astype(BF16)) + _dot(e_p.astype(BF16), vp_ref[:, hs].astype(BF16))
        o_ref[:, hs] = o / l
        lse_ref[:, hs] = jnp.broadcast_to(m + jnp.log(l), (SPAN, HEAD_DIM))


def _attn_seq(q, kv, branch):
    s, qw = q.shape
    r = DILATIONS[branch]
    hw = N_HEADS_B * HEAD_DIM
    nq = qw // hw
    nkv = kv.shape[1] // hw
    qr = q.reshape(s // r, r * qw)
    kvr = kv.reshape(s // r, r * kv.shape[1])
    kcol = 2 * branch
    blk = (SPAN, hw)
    prev = lambda c, b: jnp.maximum(b - 1, 0)
    o, lse = pl.pallas_call(
        functools.partial(_attn_seq_kernel, dilation=r, slopes=[float(v) for v in _alibi_slopes()[branch]]),
        grid=(r, s // (r * SPAN)),
        in_specs=[pl.BlockSpec(blk, lambda c, b: (b, c * nq + branch)),
                  pl.BlockSpec(blk, lambda c, b: (b, c * nkv + kcol)),
                  pl.BlockSpec(blk, lambda c, b: (prev(c, b), c * nkv + kcol)),
                  pl.BlockSpec(blk, lambda c, b: (b, c * nkv + kcol + 1)),
                  pl.BlockSpec(blk, lambda c, b: (prev(c, b), c * nkv + kcol + 1))],
        out_specs=[pl.BlockSpec(blk, lambda c, b: (b, c)), pl.BlockSpec(blk, lambda c, b: (b, c))],
        out_shape=[jax.ShapeDtypeStruct((s // r, r * hw), F32), jax.ShapeDtypeStruct((s // r, r * hw), F32)],
        compiler_params=_cparams("parallel", "arbitrary"),
        name=f"attn_seq_w{WINDOWS[branch]}",
    )(qr, kvr, kvr, kvr, kvr)
    return o.reshape(s, hw), lse.reshape(s, hw)


def _attn_step_kernel(q_ref, kvn_ref, ind_ref, indt_ref, sl_ref, *refs):
    n_br = len(WINDOWS)
    state_refs, (o_ref, lse_ref) = refs[:2 * n_br], refs[2 * n_br:]
    hw = N_HEADS_B * HEAD_DIM
    ind = ind_ref[...]
    indt = indt_ref[...]
    scale = HEAD_DIM ** -0.5

    def split_dot(a, b):
        hi = a.astype(BF16)
        lo = (a - hi.astype(F32)).astype(BF16)
        return _dot(hi, b) + _dot(lo, b)

    back = (SPAN - lax.broadcasted_iota(jnp.int32, (SPAN, LANES), 0)).astype(F32)
    for g in range(n_br):
        q = q_ref[0, :, g * hw:(g + 1) * hw].astype(F32)
        k_new = kvn_ref[0, :, 2 * g * hw:(2 * g + 1) * hw]
        v_new = kvn_ref[0, :, (2 * g + 1) * hw:(2 * g + 2) * hw]
        k_old = state_refs[2 * g][0]
        v_old = state_refs[2 * g + 1][0]
        slope = sl_ref[g:g + 1, :]
        s_old = split_dot(k_old * q, ind) * scale - slope * (back * float(DILATIONS[g]))
        s_new = split_dot(jnp.broadcast_to(k_new * q, (8, hw)), ind)[0:1, :] * scale
        m = jnp.maximum(jnp.max(s_old, axis=0, keepdims=True), s_new)
        e_old = jnp.exp(s_old - m)
        e_new = jnp.exp(s_new - m)
        l = jnp.sum(e_old, axis=0, keepdims=True) + e_new
        ex_old = split_dot(e_old, indt)
        tail = jnp.concatenate([e_new, l, m + jnp.log(l), jnp.zeros((5, LANES), F32)], axis=0)
        tail = split_dot(tail, indt)
        o = jnp.sum(ex_old * v_old, axis=0, keepdims=True) + tail[0:1, :] * v_new
        o_ref[0, :, g * hw:(g + 1) * hw] = o / tail[1:2, :]
        lse_ref[0, :, g * hw:(g + 1) * hw] = tail[2:3, :]


def _attn_step(q, kv_new, states):
    nb = states[0].shape[0]
    hw = N_HEADS_B * HEAD_DIM
    n_br = len(WINDOWS)
    ind = np.zeros((hw, LANES), np.float32)
    ind[:, :N_HEADS_B] = np.kron(np.eye(N_HEADS_B, dtype=np.float32), np.ones((HEAD_DIM, 1), np.float32))
    slopes = np.zeros((n_br, LANES), np.float32)
    slopes[:, :N_HEADS_B] = _alibi_slopes()
    slopes = jnp.asarray(slopes)
    q3 = q[:nb].reshape(nb, 1, q.shape[1])
    kv3 = kv_new[:nb].reshape(nb, 1, kv_new.shape[1])
    st_args, st_specs = [], []
    for g in range(n_br):
        r = DILATIONS[g]
        lb = states[g].shape[1]
        assert lb == WINDOWS[g], "window buffers must be full"
        st = states[g].reshape(nb, SPAN, r * 2 * hw)
        st_args += [st, st]
        st_specs += [pl.BlockSpec((1, SPAN, hw), lambda n: (n, 0, 0)), pl.BlockSpec((1, SPAN, hw), lambda n: (n, 0, 1))]
    o, lse = pl.pallas_call(
        _attn_step_kernel,
        grid=(nb,),
        in_specs=[pl.BlockSpec((1, 1, q.shape[1]), lambda n: (n, 0, 0)),
                  pl.BlockSpec((1, 1, kv_new.shape[1]), lambda n: (n, 0, 0)),
                  _full(ind.shape), _full(ind.T.shape), _full(slopes.shape)] + st_specs,
        out_specs=[pl.BlockSpec((1, 1, n_br * hw), lambda n: (n, 0, 0)),
                   pl.BlockSpec((1, 1, n_br * hw), lambda n: (n, 0, 0))],
        out_shape=[jax.ShapeDtypeStruct((nb, 1, n_br * hw), F32), jax.ShapeDtypeStruct((nb, 1, n_br * hw), F32)],
        compiler_params=_cparams("parallel"),
        name="attn_step",
    )(q3, kv3, jnp.asarray(ind, BF16), jnp.asarray(ind.T, BF16), slopes, *st_args)
    return o.reshape(nb, n_br * hw), lse.reshape(nb, n_br * hw)


def _attn_out_kernel(*refs):
    n_br = len(WINDOWS)
    o_refs, lse_refs = refs[:n_br], refs[n_br:2 * n_br]
    h_ref, wo_ref, out_ref = refs[2 * n_br:]
    lses = [r[...] for r in lse_refs]
    m = functools.reduce(jnp.maximum, lses)
    ws = [jnp.exp(l - m) for l in lses]
    tot = functools.reduce(lambda a, b: a + b, ws)
    o = functools.reduce(lambda a, b: a + b, [w / tot * r[...] for w, r in zip(ws, o_refs)])
    out_ref[...] = h_ref[...] + _dot(o.astype(BF16), wo_ref[...])


def _attn_out(os, lses, h, wo, tb):
    n, d = h.shape
    hw = wo.shape[0]
    bs = pl.BlockSpec((tb, hw), lambda i: (i, 0))
    return pl.pallas_call(
        _attn_out_kernel,
        grid=(n // tb,),
        in_specs=[bs] * (2 * len(os)) + [pl.BlockSpec((tb, d), lambda i: (i, 0)), _full(wo.shape)],
        out_specs=pl.BlockSpec((tb, d), lambda i: (i, 0)),
        out_shape=jax.ShapeDtypeStruct((n, d), F32),
        compiler_params=_cparams("parallel"),
        name="attn_out",
    )(*os, *lses, h, wo)


def kernel(x_prompt, x_sample, state_kv_w128, state_kv_w512, state_kv_w2048, p_prompt, p_sample, g_mix, sgu_w_in, sgu_g_v, sgu_w_s, sgu_b_s, sgu_w_out, kv_g, w_kv, attn_w_q, attn_w_o, g_ffn, peer_w_query, peer_subkeys, peer_u, peer_v, ple_g, ple_w_gate, ple_w_proj, g_final):
    depth, d = g_mix.shape
    assert depth == 2 and sgu_w_in.shape[0] == 1 and attn_w_q.shape[0] == 1
    b_p, s_p, _ = x_prompt.shape
    b_s, t_s, _ = x_sample.shape
    assert b_p == 1 and t_s == 1 and s_p % max(WINDOWS) == 0
    states = (state_kv_w128, state_kv_w512, state_kv_w2048)
    hw = N_HEADS_B * HEAD_DIM
    n_br = len(WINDOWS)

    row = lambda a: a.reshape(1, -1).astype(F32)
    bf = lambda a: a.astype(BF16)

    w_in, w_out = bf(sgu_w_in[0]), bf(sgu_w_out[0])
    gw = d // SGU_GROUPS
    bs_full = jnp.repeat(sgu_b_s[0].T, gw, axis=1)
    ws0 = jnp.repeat(sgu_w_s[0][:, 0, 0], gw).reshape(1, d)
    bs0 = bs_full[0:1]
    wq_peer = bf(peer_w_query)
    subkeys = bf(peer_subkeys.reshape(depth, 2 * PEER_HEADS, PEER_NKEYS, -1))
    u_tab = bf(peer_u)
    v_tab = bf(peer_v)
    w_gate, w_proj = bf(ple_w_gate), bf(ple_w_proj)
    w_q, w_kvb, w_o = bf(attn_w_q[0]), bf(w_kv), bf(attn_w_o[0])

    n_s = LANES
    xs = jnp.pad(x_sample.reshape(b_s, d), ((0, n_s - b_s), (0, 0)))
    ps = jnp.pad(p_sample.reshape(depth, b_s, -1), ((0, 0), (0, n_s - b_s), (0, 0)))
    xp = x_prompt.reshape(s_p, d)
    pp = p_prompt.reshape(depth, s_p, -1)

    def channel(h, p, i, tb_sel, tb, tbp, final):
        h = _peer(h, row(g_ffn[i]), wq_peer[i], subkeys[i], u_tab[i], v_tab[i], tb_sel, tb, 1024)
        return _ple(h, p[i], row(ple_g[i]), w_gate[i], w_proj[i], row(g_final), tbp, final)

    hp = _sgu_seq(xp, row(g_mix[0]), w_in, row(sgu_g_v[0]), sgu_w_s[0], bs_full, w_out, 512)
    hs, v_s = _sgu_first(xs, row(g_mix[0]), w_in, row(sgu_g_v[0]), ws0, bs0, w_out)
    hp = channel(hp, pp, 0, 256, 512, 512, False)
    hs = channel(hs, ps, 0, n_s, n_s, n_s, False)

    qp, kvp = _qkv(hp, row(g_mix[1]), row(kv_g), w_q, w_kvb, 512)
    qs, kvs = _qkv(hs, row(g_mix[1]), row(kv_g), w_q, w_kvb, n_s)
    outs = [_attn_seq(qp, kvp, g) for g in range(n_br)]
    hp = _attn_out([o for o, _ in outs], [l for _, l in outs], hp, w_o, 512)
    o_s, lse_s = _attn_step(qs, kvs, states)
    pad = lambda a: jnp.pad(a, ((0, n_s - b_s), (0, 0)))
    hs = _attn_out([pad(o_s[:, g * hw:(g + 1) * hw]) for g in range(n_br)],
                   [pad(lse_s[:, g * hw:(g + 1) * hw]) for g in range(n_br)], hs, w_o, n_s)
    yp = channel(hp, pp, 1, 256, 512, 512, True)
    ys = channel(hs, ps, 1, n_s, n_s, n_s, True)

    kv_p = kvp.reshape(1, s_p, n_br, 2, N_HEADS_B, HEAD_DIM)
    kv_n = kvs[:b_s].reshape(b_s, 1, n_br, 2, N_HEADS_B, HEAD_DIM)
    new_p = [kv_p[:, s_p - min(WINDOWS[g], s_p):, g] for g in range(n_br)]
    new_s = []
    for g in range(n_br):
        full_len = states[g].shape[1] + 1
        keep = min(WINDOWS[g], full_len)
        new_s.append(jnp.concatenate([states[g][:, full_len - keep:], kv_n[:, :, g]], axis=1))
    return (yp.reshape(1, s_p, d), ys[:b_s].reshape(b_s, 1, d), v_s[:b_s].reshape(1, b_s, 1, d),
            new_p[0], new_p[1], new_p[2], new_s[0], new_s[1], new_s[2])
```

```python
import functools
import math

import numpy as np
import jax
import jax.numpy as jnp
from jax import lax
from jax.experimental import pallas as pl
from jax.experimental.pallas import tpu as pltpu

F32 = jnp.float32
BF16 = jnp.bfloat16

EPS = 1e-6
NEG = -1e30
CHUNK = 128
SGU_GROUPS = 8
HEAD_DIM = 64
N_HEADS_B = 8
WINDOWS = (128, 512, 2048)
DILATIONS = (1, 4, 16)
SPAN = 128
PEER_HEADS = 8
PEER_NKEYS = 128
PEER_TOPK = 16

LANES = 128
VMEM_LIMIT = 56 * 1024 * 1024


def _cparams(*sem):
    return pltpu.CompilerParams(dimension_semantics=sem, vmem_limit_bytes=VMEM_LIMIT)


def _rms(x, g):
    return x * lax.rsqrt(jnp.mean(x * x, axis=-1, keepdims=True) + EPS) * g


def _gelu(x):
    return 0.5 * x * (1.0 + lax.erf(x * (1.0 / math.sqrt(2.0))))


def _dot(a, b):
    return jnp.dot(a, b, preferred_element_type=F32)


def _dot_nt(a, b):
    return lax.dot_general(a, b, (((1,), (1,)), ((), ())), preferred_element_type=F32)


def _full(shape):
    nd = len(shape)
    return pl.BlockSpec(shape, lambda *_: (0,) * nd)


def _sgu_front(x, gmix_ref, win_ref, gv_ref):
    d = x.shape[-1]
    xn = _rms(x, gmix_ref[...]).astype(BF16)
    uv = _gelu(_dot(xn, win_ref[...]))
    return uv[:, :d], _rms(uv[:, d:], gv_ref[...])


def _sgu_seq_kernel(x_ref, gmix_ref, win_ref, gv_ref, ws_ref, bs_ref, wout_ref, h_ref, mixed_ref):
    x = x_ref[...]
    u, v = _sgu_front(x, gmix_ref, win_ref, gv_ref)
    vb = v.astype(BF16)
    gw = v.shape[-1] // SGU_GROUPS
    row = lax.broadcasted_iota(jnp.int32, (CHUNK, CHUNK), 0)
    col = lax.broadcasted_iota(jnp.int32, (CHUNK, CHUNK), 1)
    for g in range(SGU_GROUPS):
        wsg = jnp.where(col <= row, ws_ref[g], 0.0).astype(BF16)
        for c in range(x.shape[0] // CHUNK):
            rs = slice(c * CHUNK, (c + 1) * CHUNK)
            cs = slice(g * gw, (g + 1) * gw)
            mixed_ref[rs, cs] = _dot(wsg, vb[rs, cs]) + bs_ref[:, cs]
    z = (u * mixed_ref[...]).astype(BF16)
    h_ref[...] = x + _dot(z, wout_ref[...])


def _sgu_first_kernel(x_ref, gmix_ref, win_ref, gv_ref, ws0_ref, bs0_ref, wout_ref, h_ref, v_ref):
    x = x_ref[...]
    u, v = _sgu_front(x, gmix_ref, win_ref, gv_ref)
    v_ref[...] = v
    z = (u * (v * ws0_ref[...] + bs0_ref[...])).astype(BF16)
    h_ref[...] = x + _dot(z, wout_ref[...])


def _sgu_seq(x, gmix, win, gv, ws, bs_full, wout, tb):
    n, d = x.shape
    return pl.pallas_call(
        _sgu_seq_kernel,
        grid=(n // tb,),
        in_specs=[pl.BlockSpec((tb, d), lambda i: (i, 0)), _full(gmix.shape), _full(win.shape),
                  _full(gv.shape), _full(ws.shape), _full(bs_full.shape), _full(wout.shape)],
        out_specs=pl.BlockSpec((tb, d), lambda i: (i, 0)),
        out_shape=jax.ShapeDtypeStruct((n, d), F32),
        scratch_shapes=[pltpu.VMEM((tb, d), F32)],
        compiler_params=_cparams("parallel"),
        name="sgu_seq",
    )(x, gmix, win, gv, ws, bs_full, wout)


def _sgu_first(x, gmix, win, gv, ws0, bs0, wout):
    n, d = x.shape
    return pl.pallas_call(
        _sgu_first_kernel,
        grid=(1,),
        in_specs=[_full(x.shape), _full(gmix.shape), _full(win.shape), _full(gv.shape),
                  _full(ws0.shape), _full(bs0.shape), _full(wout.shape)],
        out_specs=[_full((n, d)), _full((n, d))],
        out_shape=[jax.ShapeDtypeStruct((n, d), F32), jax.ShapeDtypeStruct((n, d), F32)],
        compiler_params=_cparams("arbitrary"),
        name="sgu_first",
    )(x, gmix, win, gv, ws0, bs0, wout)


_CAND_COUNTS = [PEER_TOPK // (i + 1) for i in range(PEER_TOPK)]
_N_CAND = sum(_CAND_COUNTS)
_N_CAND_PAD = -(-_N_CAND // 8) * 8
TOKEN_GROUP = 16
TILE_ROW_STRIDE = 24


def _extract_top(work, payloads, emit):
    iota = lax.broadcasted_iota(jnp.int32, work.shape, 0).astype(F32)
    n_rows = float(work.shape[0])
    for r in range(PEER_TOPK):
        m = jnp.max(work, axis=0, keepdims=True)
        first = jnp.min(jnp.where(work == m, iota, n_rows), axis=0, keepdims=True)
        hit = iota == first
        emit(r, m, first, [jnp.sum(jnp.where(hit, p, 0.0), axis=0, keepdims=True) for p in payloads])
        work = jnp.where(hit, -jnp.inf, work)


def _peer_select_kernel(h_ref, g_ref, wq_ref, sk_ref, xn_ref, i1_ref, i2_ref, gate_ref,
                        q_scr, tv_scr, ti_scr, top_scr, a_scr, b_scr, g_scr):
    tb = h_ref.shape[0]
    xn = _rms(h_ref[...], g_ref[...]).astype(BF16)
    xn_ref[...] = xn
    q_scr[...] = _dot(xn, wq_ref[...])
    dk = sk_ref.shape[-1]

    def head_body(h, carry):
        for hc in (2 * h, 2 * h + 1):
            qc = q_scr[:, pl.ds(pl.multiple_of(hc * dk, dk), dk)].astype(BF16)
            sc = _dot_nt(sk_ref[hc], qc)

            def emit(r, m, first, _, hc=hc):
                tv_scr[hc, r:r + 1, :] = m
                ti_scr[hc, r:r + 1, :] = first

            _extract_top(sc, [], emit)

        t1, t2 = tv_scr[2 * h], tv_scr[2 * h + 1]
        k1, k2 = ti_scr[2 * h], ti_scr[2 * h + 1]
        pad = _N_CAND_PAD - _N_CAND
        cand = jnp.concatenate([t1[i:i + 1, :] + t2[:nj, :] for i, nj in enumerate(_CAND_COUNTS)]
                               + [jnp.full((pad, tb), -jnp.inf, F32)], axis=0)
        ka = jnp.concatenate([jnp.broadcast_to(k1[i:i + 1, :], (nj, tb)) for i, nj in enumerate(_CAND_COUNTS)]
                             + [jnp.zeros((pad, tb), F32)], axis=0)
        kb = jnp.concatenate([k2[:nj, :] for nj in _CAND_COUNTS] + [jnp.zeros((pad, tb), F32)], axis=0)

        def emit(r, m, first, picked):
            top_scr[r:r + 1, :] = m
            a_scr[h, r:r + 1, :] = picked[0]
            b_scr[h, r:r + 1, :] = picked[1]

        _extract_top(cand, [ka, kb], emit)
        e = jnp.exp(top_scr[...] - top_scr[0:1, :])
        g_scr[h] = e / jnp.sum(e, axis=0, keepdims=True)
        return carry

    lax.fori_loop(0, PEER_HEADS, head_body, 0)
    n_sel = PEER_HEADS * PEER_TOPK
    i1_ref[...] = a_scr[...].reshape(n_sel, tb).T
    i2_ref[...] = b_scr[...].reshape(n_sel, tb).T
    gate_ref[...] = g_scr[...].reshape(n_sel, tb).T


def _peer_select(h, g, wq, sk, tb):
    n, d = h.shape
    nhc, nk, dk = sk.shape
    n_sel = PEER_HEADS * PEER_TOPK
    sel = pl.BlockSpec((tb, n_sel), lambda i: (i, 0))
    return pl.pallas_call(
        _peer_select_kernel,
        grid=(n // tb,),
        in_specs=[pl.BlockSpec((tb, d), lambda i: (i, 0)), _full(g.shape), _full(wq.shape), _full(sk.shape)],
        out_specs=[pl.BlockSpec((tb, d), lambda i: (i, 0)), sel, sel, sel],
        out_shape=[jax.ShapeDtypeStruct((n, d), BF16)] + [jax.ShapeDtypeStruct((n, n_sel), F32)] * 3,
        scratch_shapes=[pltpu.VMEM((tb, wq.shape[1]), F32),
                        pltpu.VMEM((nhc, PEER_TOPK, tb), F32),
                        pltpu.VMEM((nhc, PEER_TOPK, tb), F32),
                        pltpu.VMEM((PEER_TOPK, tb), F32),
                        pltpu.VMEM((PEER_HEADS, PEER_TOPK, tb), F32),
                        pltpu.VMEM((PEER_HEADS, PEER_TOPK, tb), F32),
                        pltpu.VMEM((PEER_HEADS, PEER_TOPK, tb), F32)],
        compiler_params=_cparams("parallel"),
        name="peer_select",
    )(h, g, wq, sk)


def _build_gates(i1_ref, i2_ref, g_ref, gates_ref, tile_ref):
    tb = i1_ref.shape[0]
    nk = PEER_NKEYS
    sub = lax.broadcasted_iota(jnp.int32, (nk, i1_ref.shape[1]), 0).astype(F32)

    def group_body(gi, carry):
        t0 = pl.multiple_of(gi * TOKEN_GROUP, TOKEN_GROUP)
        for tt in range(TOKEN_GROUP):
            row = pl.ds(t0 + tt, 1)
            first = jnp.where(sub == i1_ref[row, :], g_ref[row, :], 0.0).astype(BF16)
            second = jnp.where(sub == i2_ref[row, :], 1.0, 0.0).astype(BF16)
            tile_ref[pl.ds(tt, nk, stride=TILE_ROW_STRIDE), :] = _dot_nt(first, second)
        for a in range(nk):
            rows = tile_ref[a * TILE_ROW_STRIDE:a * TILE_ROW_STRIDE + TOKEN_GROUP, :]
            gates_ref[pl.ds(t0, TOKEN_GROUP), a * nk:(a + 1) * nk] = rows.astype(BF16)
        return carry

    lax.fori_loop(0, tb // TOKEN_GROUP, group_body, 0)


def _peer_dense_kernel(xn_ref, i1_ref, i2_ref, g_ref, u_ref, v_ref, h_ref, o_ref, acc_ref, gates_ref, tile_ref):
    j = pl.program_id(1)
    eb = u_ref.shape[0]

    @pl.when(j == 0)
    def _():
        acc_ref[...] = jnp.zeros_like(acc_ref)
        _build_gates(i1_ref, i2_ref, g_ref, gates_ref, tile_ref)

    act = _gelu(_dot_nt(xn_ref[...], u_ref[...]))
    gates = gates_ref[:, pl.ds(pl.multiple_of(j * eb, eb), eb)]
    acc_ref[...] += _dot((gates.astype(F32) * act).astype(BF16), v_ref[...])

    @pl.when(j == pl.num_programs(1) - 1)
    def _():
        o_ref[...] = h_ref[...] + acc_ref[...]


def _peer_dense(xn, i1, i2, gate, u, v, h, tb, eb):
    n, d = h.shape
    n_exp = u.shape[0]
    tok = lambda w: pl.BlockSpec((tb, w), lambda i, j: (i, 0))
    return pl.pallas_call(
        _peer_dense_kernel,
        grid=(n // tb, n_exp // eb),
        in_specs=[tok(d), tok(i1.shape[1]), tok(i1.shape[1]), tok(i1.shape[1]),
                  pl.BlockSpec((eb, d), lambda i, j: (j, 0)),
                  pl.BlockSpec((eb, d), lambda i, j: (j, 0)),
                  tok(d)],
        out_specs=tok(d),
        out_shape=jax.ShapeDtypeStruct((n, d), F32),
        scratch_shapes=[pltpu.VMEM((tb, d), F32), pltpu.VMEM((tb, n_exp), BF16),
                        pltpu.VMEM((PEER_NKEYS * TILE_ROW_STRIDE, PEER_NKEYS), F32)],
        compiler_params=_cparams("parallel", "arbitrary"),
        name="peer_dense",
    )(xn, i1, i2, gate, u, v, h)


def _peer(h, g, wq, sk, u, v, tb_sel, tb, eb):
    xn, i1, i2, gate = _peer_select(h, g, wq, sk, tb_sel)
    return _peer_dense(xn, i1, i2, gate, u, v, h, tb, eb)


def _ple_kernel(h_ref, p_ref, g_ref, wg_ref, wp_ref, gf_ref, o_ref, *, final):
    h = h_ref[...]
    gate = jax.nn.sigmoid(_dot(_rms(h, g_ref[...]).astype(BF16), wg_ref[...]))
    h = h + gate * _dot(p_ref[...].astype(BF16), wp_ref[...])
    o_ref[...] = _rms(h, gf_ref[...]) if final else h


def _ple(h, p, g, wg, wp, gf, tb, final):
    n, d = h.shape
    return pl.pallas_call(
        functools.partial(_ple_kernel, final=final),
        grid=(n // tb,),
        in_specs=[pl.BlockSpec((tb, d), lambda i: (i, 0)), pl.BlockSpec((tb, p.shape[1]), lambda i: (i, 0)),
                  _full(g.shape), _full(wg.shape), _full(wp.shape), _full(gf.shape)],
        out_specs=pl.BlockSpec((tb, d), lambda i: (i, 0)),
        out_shape=jax.ShapeDtypeStruct((n, d), F32),
        compiler_params=_cparams("parallel"),
        name="ple_final" if final else "ple",
    )(h, p, g, wg, wp, gf)


def _qkv_kernel(h_ref, gm_ref, gkv_ref, wq_ref, wkv_ref, q_ref, kv_ref):
    h = h_ref[...]
    q_ref[...] = _dot(_rms(h, gm_ref[...]).astype(BF16), wq_ref[...])
    kv_ref[...] = _dot(_rms(h, gkv_ref[...]).astype(BF16), wkv_ref[...])


def _qkv(h, gm, gkv, wq, wkv, tb):
    n, d = h.shape
    return pl.pallas_call(
        _qkv_kernel,
        grid=(n // tb,),
        in_specs=[pl.BlockSpec((tb, d), lambda i: (i, 0)), _full(gm.shape), _full(gkv.shape),
                  _full(wq.shape), _full(wkv.shape)],
        out_specs=[pl.BlockSpec((tb, wq.shape[1]), lambda i: (i, 0)),
                   pl.BlockSpec((tb, wkv.shape[1]), lambda i: (i, 0))],
        out_shape=[jax.ShapeDtypeStruct((n, wq.shape[1]), F32),
                   jax.ShapeDtypeStruct((n, wkv.shape[1]), F32)],
        compiler_params=_cparams("parallel"),
        name="qkv",
    )(h, gm, gkv, wq, wkv)


def _qkv_strided_kernel(h_ref, gm_ref, gkv_ref, wq_ref, wkv_ref, *refs):
    n_br = len(DILATIONS)
    tail_ref, outs, (q_scr, kv_scr) = refs[0], refs[1:1 + 3 * n_br], refs[1 + 3 * n_br:]
    tb = h_ref.shape[0]
    hw = N_HEADS_B * HEAD_DIM
    tiles = hw // LANES
    h = h_ref[...]
    q = _dot(_rms(h, gm_ref[...]).astype(BF16), wq_ref[...])
    kv = _dot(_rms(h, gkv_ref[...]).astype(BF16), wkv_ref[...])
    tail_ref[...] = kv
    for j in range(q_scr.shape[0]):
        q_scr[j] = q[:, j * LANES:(j + 1) * LANES]
    for j in range(kv_scr.shape[0]):
        kv_scr[j] = kv[:, j * LANES:(j + 1) * LANES]
    for g, r in enumerate(DILATIONS):
        q_ref, k_ref, v_ref = outs[3 * g:3 * g + 3]
        for c in range(r):
            rows = pl.ds(c, tb // r, stride=r)
            for j in range(tiles):
                cols = slice(j * LANES, (j + 1) * LANES)
                q_ref[c, :, cols] = q_scr[g * tiles + j, rows, :].astype(BF16)
                k_ref[c, :, cols] = kv_scr[2 * g * tiles + j, rows, :].astype(BF16)
                v_ref[c, :, cols] = kv_scr[(2 * g + 1) * tiles + j, rows, :].astype(BF16)


def _qkv_strided(h, gm, gkv, wq, wkv, tb, tail_rows):
    n, d = h.shape
    hw = N_HEADS_B * HEAD_DIM
    first_tail_block = (n - tail_rows) // tb
    out_specs = [pl.BlockSpec((tb, wkv.shape[1]), lambda i: (jnp.maximum(i - first_tail_block, 0), 0))]
    out_shape = [jax.ShapeDtypeStruct((tail_rows, wkv.shape[1]), F32)]
    for r in DILATIONS:
        out_specs += [pl.BlockSpec((r, tb // r, hw), lambda i: (0, i, 0))] * 3
        out_shape += [jax.ShapeDtypeStruct((r, n // r, hw), BF16)] * 3
    outs = pl.pallas_call(
        _qkv_strided_kernel,
        grid=(n // tb,),
        in_specs=[pl.BlockSpec((tb, d), lambda i: (i, 0)), _full(gm.shape), _full(gkv.shape),
                  _full(wq.shape), _full(wkv.shape)],
        out_specs=out_specs,
        out_shape=out_shape,
        scratch_shapes=[pltpu.VMEM((wq.shape[1] // LANES, tb, LANES), F32),
                        pltpu.VMEM((wkv.shape[1] // LANES, tb, LANES), F32)],
        compiler_params=_cparams("arbitrary"),
        name="qkv_strided",
    )(h, gm, gkv, wq, wkv)
    return outs[0], [outs[1 + 3 * g:4 + 3 * g] for g in range(len(DILATIONS))]


def _alibi_slopes():
    n = len(WINDOWS) * N_HEADS_B
    e = (np.arange(n, dtype=np.float32) + 1.0) * np.float32(8.0 / n)
    return np.exp2(-e).astype(np.float32).reshape(len(WINDOWS), N_HEADS_B)


def _attn_seq_kernel(q_ref, kc_ref, kp_ref, vc_ref, vp_ref, o_ref, lse_ref, *, dilation, slopes):
    b = pl.program_id(1)
    qi = lax.broadcasted_iota(jnp.int32, (SPAN, SPAN), 0)
    ki = lax.broadcasted_iota(jnp.int32, (SPAN, SPAN), 1)
    dist_c = (qi - ki).astype(F32) * float(dilation)
    dist_p = (qi - ki + SPAN).astype(F32) * float(dilation)
    valid_c = ki <= qi
    valid_p = jnp.logical_and(ki >= qi, b > 0)
    scale = HEAD_DIM ** -0.5
    for h in range(N_HEADS_B):
        hs = slice(h * HEAD_DIM, (h + 1) * HEAD_DIM)
        qh = q_ref[:, hs]
        s_c = _dot_nt(qh, kc_ref[:, hs]) * scale - slopes[h] * dist_c
        s_p = _dot_nt(qh, kp_ref[:, hs]) * scale - slopes[h] * dist_p
        s_c = jnp.where(valid_c, s_c, NEG)
        s_p = jnp.where(valid_p, s_p, NEG)
        m = jnp.maximum(jnp.max(s_c, axis=-1, keepdims=True), jnp.max(s_p, axis=-1, keepdims=True))
        e_c = jnp.exp(s_c - m)
        e_p = jnp.exp(s_p - m)
        l = jnp.sum(e_c, axis=-1, keepdims=True) + jnp.sum(e_p, axis=-1, keepdims=True)
        o = _dot(e_c.astype(BF16), vc_ref[:, hs]) + _dot(e_p.astype(BF16), vp_ref[:, hs])
        o_ref[:, hs] = o / l
        lse_ref[:, hs] = jnp.broadcast_to(m + jnp.log(l), (SPAN, HEAD_DIM))


def _attn_seq(q, k, v, branch):
    r, length, hw = q.shape
    cur = pl.BlockSpec((None, SPAN, hw), lambda c, b: (c, b, 0))
    prev = pl.BlockSpec((None, SPAN, hw), lambda c, b: (c, jnp.maximum(b - 1, 0), 0))
    return pl.pallas_call(
        functools.partial(_attn_seq_kernel, dilation=r, slopes=[float(s) for s in _alibi_slopes()[branch]]),
        grid=(r, length // SPAN),
        in_specs=[cur, cur, prev, cur, prev],
        out_specs=[cur, cur],
        out_shape=[jax.ShapeDtypeStruct((r, length, hw), F32)] * 2,
        compiler_params=_cparams("parallel", "arbitrary"),
        name=f"attn_seq_w{WINDOWS[branch]}",
    )(q, k, k, v, v)


def _attn_step_kernel(q_ref, kvn_ref, sl_ref, *refs):
    n_br = len(WINDOWS)
    state_refs, (o_ref, lse_ref) = refs[:n_br], refs[n_br:]
    scale = HEAD_DIM ** -0.5
    back = (SPAN - lax.broadcasted_iota(jnp.int32, (SPAN, N_HEADS_B, 1), 0)).astype(F32)
    for g in range(n_br):
        q = q_ref[0, g]
        k_new, v_new = kvn_ref[0, g, 0], kvn_ref[0, g, 1]
        k_old, v_old = state_refs[g][0, :, 0, 0], state_refs[g][0, :, 0, 1]
        slope = sl_ref[g]
        s_old = (jnp.sum(k_old * q[None], axis=-1, keepdims=True) * scale
                 - slope[None] * (back * float(DILATIONS[g])))
        s_new = jnp.sum(k_new * q, axis=-1, keepdims=True) * scale
        m = jnp.maximum(jnp.max(s_old, axis=0), s_new)
        e_old = jnp.exp(s_old - m[None])
        e_new = jnp.exp(s_new - m)
        l = jnp.sum(e_old, axis=0) + e_new
        o = (jnp.sum(e_old * v_old, axis=0) + e_new * v_new) / l
        o_ref[0, g] = o
        lse_ref[0, g] = jnp.broadcast_to(m + jnp.log(l), o.shape)


def _attn_step(q, kv_new, states):
    nb, n_br = q.shape[:2]
    slopes = jnp.asarray(_alibi_slopes()).reshape(n_br, N_HEADS_B, 1)
    st_args, st_specs = [], []
    for g in range(n_br):
        r = DILATIONS[g]
        assert states[g].shape[1] == WINDOWS[g], "window buffers must be full"
        st_args.append(states[g].reshape(nb, SPAN, r, 2, N_HEADS_B, HEAD_DIM))
        st_specs.append(pl.BlockSpec((1, SPAN, 1, 2, N_HEADS_B, HEAD_DIM), lambda n: (n, 0, 0, 0, 0, 0)))
    per_seq = lambda a: pl.BlockSpec((1,) + a.shape[1:], lambda n: (n,) + (0,) * (a.ndim - 1))
    return pl.pallas_call(
        _attn_step_kernel,
        grid=(nb,),
        in_specs=[per_seq(q), per_seq(kv_new), _full(slopes.shape)] + st_specs,
        out_specs=[per_seq(q), per_seq(q)],
        out_shape=[jax.ShapeDtypeStruct(q.shape, F32)] * 2,
        compiler_params=_cparams("parallel"),
        name="attn_step",
    )(q, kv_new, slopes, *st_args)


def _attn_out_kernel(*refs, dilations):
    n_br = len(dilations)
    o_refs, lse_refs = refs[:n_br], refs[n_br:2 * n_br]
    h_ref, wo_ref, out_ref = refs[2 * n_br:2 * n_br + 3]
    scratch = refs[2 * n_br + 3:]
    tb = h_ref.shape[0]

    def token_order(ref, scr_ref, r):
        if r == 1:
            return ref[0]
        tiles = scr_ref.shape[0]
        for c in range(r):
            for j in range(tiles):
                scr_ref[j, pl.ds(c, tb // r, stride=r), :] = ref[c, :, j * LANES:(j + 1) * LANES]
        return jnp.concatenate([scr_ref[j] for j in range(tiles)], axis=1)

    lses = [token_order(ref, scratch[g], dilations[g]) for g, ref in enumerate(lse_refs)]
    m = functools.reduce(jnp.maximum, lses)
    ws = [jnp.exp(l - m) for l in lses]
    tot = functools.reduce(lambda a, b: a + b, ws)
    o = functools.reduce(lambda a, b: a + b, [w / tot * token_order(ref, scratch[n_br + g], dilations[g])
                                              for g, (w, ref) in enumerate(zip(ws, o_refs))])
    out_ref[...] = h_ref[...] + _dot(o.astype(BF16), wo_ref[...])


def _attn_out(os, lses, h, wo, tb, dilations):
    n, d = h.shape
    hw = wo.shape[0]
    specs = [pl.BlockSpec((r, tb // r, hw), lambda i: (0, i, 0)) for r in dilations]
    return pl.pallas_call(
        functools.partial(_attn_out_kernel, dilations=dilations),
        grid=(n // tb,),
        in_specs=specs + specs + [pl.BlockSpec((tb, d), lambda i: (i, 0)), _full(wo.shape)],
        out_specs=pl.BlockSpec((tb, d), lambda i: (i, 0)),
        out_shape=jax.ShapeDtypeStruct((n, d), F32),
        scratch_shapes=[pltpu.VMEM((hw // LANES, tb, LANES), F32)] * (2 * len(dilations)),
        compiler_params=_cparams("parallel"),
        name="attn_out",
    )(*os, *lses, h, wo)


SHIFT_PARTS = 8


def _shift_kernel(*refs, plan):
    n_br = len(plan)
    states, news, outs, sem = refs[:n_br], refs[n_br:2 * n_br], refs[2 * n_br:3 * n_br], refs[3 * n_br]
    copies = []
    for g, (skip, kept) in enumerate(plan):
        nb = states[g].shape[0]
        per = nb // SHIFT_PARTS
        for part in range(SHIFT_PARTS if kept else 0):
            seqs = pl.ds(part * per, per)
            copies.append(pltpu.make_async_copy(states[g].at[seqs, pl.ds(skip, kept)],
                                                outs[g].at[seqs, pl.ds(0, kept)], sem.at[len(copies)]))
        copies.append(pltpu.make_async_copy(news[g], outs[g].at[:, pl.ds(kept, news[g].shape[1])],
                                            sem.at[len(copies)]))
    for cp in copies:
        cp.start()
    for cp in copies:
        cp.wait()


def _shift_states(states, news):
    plan, out_shape = [], []
    for g, (st, new) in enumerate(zip(states, news)):
        full_len = st.shape[1] + new.shape[1]
        keep = min(WINDOWS[g], full_len)
        kept = keep - new.shape[1]
        assert kept >= 0 and st.shape[0] % SHIFT_PARTS == 0
        plan.append((st.shape[1] - kept, kept))
        out_shape.append(jax.ShapeDtypeStruct((st.shape[0], keep) + st.shape[2:], st.dtype))
    n_copies = sum((SHIFT_PARTS if kept else 0) + 1 for _, kept in plan)
    anywhere = pl.BlockSpec(memory_space=pl.ANY)
    return pl.pallas_call(
        functools.partial(_shift_kernel, plan=plan),
        in_specs=[anywhere] * (2 * len(states)),
        out_specs=[anywhere] * len(states),
        out_shape=out_shape,
        scratch_shapes=[pltpu.SemaphoreType.DMA((n_copies,))],
        name="shift_states",
    )(*states, *news)


def kernel(x_prompt, x_sample, state_kv_w128, state_kv_w512, state_kv_w2048, p_prompt, p_sample, g_mix, sgu_w_in, sgu_g_v, sgu_w_s, sgu_b_s, sgu_w_out, kv_g, w_kv, attn_w_q, attn_w_o, g_ffn, peer_w_query, peer_subkeys, peer_u, peer_v, ple_g, ple_w_gate, ple_w_proj, g_final):
    depth, d = g_mix.shape
    assert depth == 2 and sgu_w_in.shape[0] == 1 and attn_w_q.shape[0] == 1
    b_p, s_p, _ = x_prompt.shape
    b_s, t_s, _ = x_sample.shape
    assert b_p == 1 and t_s == 1 and s_p % max(WINDOWS) == 0
    states = (state_kv_w128, state_kv_w512, state_kv_w2048)
    hw = N_HEADS_B * HEAD_DIM
    n_br = len(WINDOWS)

    row = lambda a: a.reshape(1, -1).astype(F32)
    bf = lambda a: a.astype(BF16)

    w_in, w_out = bf(sgu_w_in[0]), bf(sgu_w_out[0])
    gw = d // SGU_GROUPS
    bs_full = jnp.repeat(sgu_b_s[0].T, gw, axis=1)
    ws0 = jnp.repeat(sgu_w_s[0][:, 0, 0], gw).reshape(1, d)
    bs0 = bs_full[0:1]
    wq_peer = bf(peer_w_query)
    subkeys = bf(peer_subkeys.reshape(depth, 2 * PEER_HEADS, PEER_NKEYS, -1))
    u_tab = bf(peer_u)
    v_tab = bf(peer_v)
    w_gate, w_proj = bf(ple_w_gate), bf(ple_w_proj)
    w_q, w_kvb, w_o = bf(attn_w_q[0]), bf(w_kv), bf(attn_w_o[0])

    n_s = LANES
    xs = jnp.pad(x_sample.reshape(b_s, d), ((0, n_s - b_s), (0, 0)))
    ps = jnp.pad(p_sample.reshape(depth, b_s, -1), ((0, 0), (0, n_s - b_s), (0, 0)))
    xp = x_prompt.reshape(s_p, d)
    pp = p_prompt.reshape(depth, s_p, -1)

    def channel(h, p, i, tb_sel, tb, tbp, final):
        h = _peer(h, row(g_ffn[i]), wq_peer[i], subkeys[i], u_tab[i], v_tab[i], tb_sel, tb, 1024)
        return _ple(h, p[i], row(ple_g[i]), w_gate[i], w_proj[i], row(g_final), tbp, final)

    hp = _sgu_seq(xp, row(g_mix[0]), w_in, row(sgu_g_v[0]), sgu_w_s[0], bs_full, w_out, 512)
    hs, v_s = _sgu_first(xs, row(g_mix[0]), w_in, row(sgu_g_v[0]), ws0, bs0, w_out)
    hp = channel(hp, pp, 0, 256, 512, 512, False)
    hs = channel(hs, ps, 0, n_s, n_s, n_s, False)

    tail = min(max(WINDOWS), s_p)
    kv_tail, qkv_p = _qkv_strided(hp, row(g_mix[1]), row(kv_g), w_q, w_kvb, 512, tail)
    qs, kvs = _qkv(hs, row(g_mix[1]), row(kv_g), w_q, w_kvb, n_s)
    outs = [_attn_seq(*qkv_p[g], g) for g in range(n_br)]
    hp = _attn_out([o for o, _ in outs], [l for _, l in outs], hp, w_o, 512, DILATIONS)
    q_step = qs[:b_s].reshape(b_s, n_br, N_HEADS_B, HEAD_DIM)
    kv_step = kvs[:b_s].reshape(b_s, n_br, 2, N_HEADS_B, HEAD_DIM)
    o_s, lse_s = _attn_step(q_step, kv_step, states)
    pad = lambda a, g: jnp.pad(a[:, g].reshape(1, b_s, hw), ((0, 0), (0, n_s - b_s), (0, 0)))
    hs = _attn_out([pad(o_s, g) for g in range(n_br)], [pad(lse_s, g) for g in range(n_br)],
                   hs, w_o, n_s, (1,) * n_br)
    yp = channel(hp, pp, 1, 256, 512, 512, True)
    ys = channel(hs, ps, 1, n_s, n_s, n_s, True)

    kv_p = kv_tail.reshape(1, tail, n_br, 2, N_HEADS_B, HEAD_DIM)
    new_p = [kv_p[:, tail - min(WINDOWS[g], s_p):, g] for g in range(n_br)]
    new_s = _shift_states(states, [kv_step[:, g][:, None] for g in range(n_br)])
    return (yp.reshape(1, s_p, d), ys[:b_s].reshape(b_s, 1, d), v_s[:b_s].reshape(1, b_s, 1, d),
            new_p[0], new_p[1], new_p[2], new_s[0], new_s[1], new_s[2])
```

```python
import functools
import math

import numpy as np
import jax
import jax.numpy as jnp
from jax import lax
from jax.experimental import pallas as pl
from jax.experimental.pallas import tpu as pltpu

F32 = jnp.float32
BF16 = jnp.bfloat16

EPS = 1e-6
NEG = -1e30
CHUNK = 128
SGU_GROUPS = 8
HEAD_DIM = 64
N_HEADS_B = 8
WINDOWS = (128, 512, 2048)
DILATIONS = (1, 4, 16)
SPAN = 128
PEER_HEADS = 8
PEER_NKEYS = 128
PEER_TOPK = 16

LANES = 128
VMEM_LIMIT = 56 * 1024 * 1024


def _cparams(*sem):
    return pltpu.CompilerParams(dimension_semantics=sem, vmem_limit_bytes=VMEM_LIMIT)


def _rms(x, g):
    return x * lax.rsqrt(jnp.mean(x * x, axis=-1, keepdims=True) + EPS) * g


def _gelu(x):
    return 0.5 * x * (1.0 + lax.erf(x * (1.0 / math.sqrt(2.0))))


def _dot(a, b):
    return jnp.dot(a, b, preferred_element_type=F32)


def _dot_nt(a, b):
    return lax.dot_general(a, b, (((1,), (1,)), ((), ())), preferred_element_type=F32)


def _full(shape):
    nd = len(shape)
    return pl.BlockSpec(shape, lambda *_: (0,) * nd)


def _sgu_front(x, gmix_ref, win_ref, gv_ref):
    d = x.shape[-1]
    xn = _rms(x, gmix_ref[...]).astype(BF16)
    uv = _gelu(_dot(xn, win_ref[...]))
    return uv[:, :d], _rms(uv[:, d:], gv_ref[...])


def _sgu_seq_kernel(x_ref, gmix_ref, win_ref, gv_ref, ws_ref, bs_ref, wout_ref, h_ref, mixed_ref):
    x = x_ref[...]
    u, v = _sgu_front(x, gmix_ref, win_ref, gv_ref)
    vb = v.astype(BF16)
    gw = v.shape[-1] // SGU_GROUPS
    row = lax.broadcasted_iota(jnp.int32, (CHUNK, CHUNK), 0)
    col = lax.broadcasted_iota(jnp.int32, (CHUNK, CHUNK), 1)
    for g in range(SGU_GROUPS):
        wsg = jnp.where(col <= row, ws_ref[g], 0.0).astype(BF16)
        for c in range(x.shape[0] // CHUNK):
            rs = slice(c * CHUNK, (c + 1) * CHUNK)
            cs = slice(g * gw, (g + 1) * gw)
            mixed_ref[rs, cs] = _dot(wsg, vb[rs, cs]) + bs_ref[:, cs]
    z = (u * mixed_ref[...]).astype(BF16)
    h_ref[...] = x + _dot(z, wout_ref[...])


def _sgu_first_kernel(x_ref, gmix_ref, win_ref, gv_ref, ws0_ref, bs0_ref, wout_ref, h_ref, v_ref):
    x = x_ref[...]
    u, v = _sgu_front(x, gmix_ref, win_ref, gv_ref)
    v_ref[...] = v
    z = (u * (v * ws0_ref[...] + bs0_ref[...])).astype(BF16)
    h_ref[...] = x + _dot(z, wout_ref[...])


def _sgu_seq(x, gmix, win, gv, ws, bs_full, wout, tb):
    n, d = x.shape
    return pl.pallas_call(
        _sgu_seq_kernel,
        grid=(n // tb,),
        in_specs=[pl.BlockSpec((tb, d), lambda i: (i, 0)), _full(gmix.shape), _full(win.shape),
                  _full(gv.shape), _full(ws.shape), _full(bs_full.shape), _full(wout.shape)],
        out_specs=pl.BlockSpec((tb, d), lambda i: (i, 0)),
        out_shape=jax.ShapeDtypeStruct((n, d), F32),
        scratch_shapes=[pltpu.VMEM((tb, d), F32)],
        compiler_params=_cparams("parallel"),
        name="sgu_seq",
    )(x, gmix, win, gv, ws, bs_full, wout)


def _sgu_first(x, gmix, win, gv, ws0, bs0, wout):
    n, d = x.shape
    return pl.pallas_call(
        _sgu_first_kernel,
        grid=(1,),
        in_specs=[_full(x.shape), _full(gmix.shape), _full(win.shape), _full(gv.shape),
                  _full(ws0.shape), _full(bs0.shape), _full(wout.shape)],
        out_specs=[_full((n, d)), _full((n, d))],
        out_shape=[jax.ShapeDtypeStruct((n, d), F32), jax.ShapeDtypeStruct((n, d), F32)],
        compiler_params=_cparams("arbitrary"),
        name="sgu_first",
    )(x, gmix, win, gv, ws0, bs0, wout)


_CAND_COUNTS = [PEER_TOPK // (i + 1) for i in range(PEER_TOPK)]
_N_CAND = sum(_CAND_COUNTS)
_N_CAND_PAD = -(-_N_CAND // 8) * 8
TOKEN_GROUP = 16
TILE_ROW_STRIDE = 24


def _extract_top(work, payloads, emit):
    iota = lax.broadcasted_iota(jnp.int32, work.shape, 0).astype(F32)
    n_rows = float(work.shape[0])
    for r in range(PEER_TOPK):
        m = jnp.max(work, axis=0, keepdims=True)
        first = jnp.min(jnp.where(work == m, iota, n_rows), axis=0, keepdims=True)
        hit = iota == first
        emit(r, m, first, [jnp.sum(jnp.where(hit, p, 0.0), axis=0, keepdims=True) for p in payloads])
        work = jnp.where(hit, -jnp.inf, work)


def _peer_select_kernel(h_ref, g_ref, wq_ref, sk_ref, xn_ref, i1_ref, i2_ref, gate_ref,
                        q_scr, tv_scr, ti_scr, top_scr, a_scr, b_scr, g_scr):
    tb = h_ref.shape[0]
    xn = _rms(h_ref[...], g_ref[...]).astype(BF16)
    xn_ref[...] = xn
    q_scr[...] = _dot(xn, wq_ref[...])
    dk = sk_ref.shape[-1]

    def head_body(h, carry):
        for hc in (2 * h, 2 * h + 1):
            qc = q_scr[:, pl.ds(pl.multiple_of(hc * dk, dk), dk)].astype(BF16)
            sc = _dot_nt(sk_ref[hc], qc)

            def emit(r, m, first, _, hc=hc):
                tv_scr[hc, r:r + 1, :] = m
                ti_scr[hc, r:r + 1, :] = first

            _extract_top(sc, [], emit)

        t1, t2 = tv_scr[2 * h], tv_scr[2 * h + 1]
        k1, k2 = ti_scr[2 * h], ti_scr[2 * h + 1]
        pad = _N_CAND_PAD - _N_CAND
        cand = jnp.concatenate([t1[i:i + 1, :] + t2[:nj, :] for i, nj in enumerate(_CAND_COUNTS)]
                               + [jnp.full((pad, tb), -jnp.inf, F32)], axis=0)
        ka = jnp.concatenate([jnp.broadcast_to(k1[i:i + 1, :], (nj, tb)) for i, nj in enumerate(_CAND_COUNTS)]
                             + [jnp.zeros((pad, tb), F32)], axis=0)
        kb = jnp.concatenate([k2[:nj, :] for nj in _CAND_COUNTS] + [jnp.zeros((pad, tb), F32)], axis=0)

        def emit(r, m, first, picked):
            top_scr[r:r + 1, :] = m
            a_scr[h, r:r + 1, :] = picked[0]
            b_scr[h, r:r + 1, :] = picked[1]

        _extract_top(cand, [ka, kb], emit)
        e = jnp.exp(top_scr[...] - top_scr[0:1, :])
        g_scr[h] = e / jnp.sum(e, axis=0, keepdims=True)
        return carry

    lax.fori_loop(0, PEER_HEADS, head_body, 0)
    n_sel = PEER_HEADS * PEER_TOPK
    i1_ref[...] = a_scr[...].reshape(n_sel, tb).T
    i2_ref[...] = b_scr[...].reshape(n_sel, tb).T
    gate_ref[...] = g_scr[...].reshape(n_sel, tb).T


def _peer_select(h, g, wq, sk, tb):
    n, d = h.shape
    nhc, nk, dk = sk.shape
    n_sel = PEER_HEADS * PEER_TOPK
    sel = pl.BlockSpec((tb, n_sel), lambda i: (i, 0))
    return pl.pallas_call(
        _peer_select_kernel,
        grid=(n // tb,),
        in_specs=[pl.BlockSpec((tb, d), lambda i: (i, 0)), _full(g.shape), _full(wq.shape), _full(sk.shape)],
        out_specs=[pl.BlockSpec((tb, d), lambda i: (i, 0)), sel, sel, sel],
        out_shape=[jax.ShapeDtypeStruct((n, d), BF16)] + [jax.ShapeDtypeStruct((n, n_sel), F32)] * 3,
        scratch_shapes=[pltpu.VMEM((tb, wq.shape[1]), F32),
                        pltpu.VMEM((nhc, PEER_TOPK, tb), F32),
                        pltpu.VMEM((nhc, PEER_TOPK, tb), F32),
                        pltpu.VMEM((PEER_TOPK, tb), F32),
                        pltpu.VMEM((PEER_HEADS, PEER_TOPK, tb), F32),
                        pltpu.VMEM((PEER_HEADS, PEER_TOPK, tb), F32),
                        pltpu.VMEM((PEER_HEADS, PEER_TOPK, tb), F32)],
        compiler_params=_cparams("parallel"),
        name="peer_select",
    )(h, g, wq, sk)


def _build_gates(i1_ref, i2_ref, g_ref, gates_ref, tile_ref):
    tb = i1_ref.shape[0]
    nk = PEER_NKEYS
    sub = lax.broadcasted_iota(jnp.int32, (nk, i1_ref.shape[1]), 0).astype(F32)

    def group_body(gi, carry):
        t0 = pl.multiple_of(gi * TOKEN_GROUP, TOKEN_GROUP)
        for tt in range(TOKEN_GROUP):
            row = pl.ds(t0 + tt, 1)
            first = jnp.where(sub == i1_ref[row, :], g_ref[row, :], 0.0).astype(BF16)
            second = jnp.where(sub == i2_ref[row, :], 1.0, 0.0).astype(BF16)
            tile_ref[pl.ds(tt, nk, stride=TILE_ROW_STRIDE), :] = _dot_nt(first, second)
        for a in range(nk):
            rows = tile_ref[a * TILE_ROW_STRIDE:a * TILE_ROW_STRIDE + TOKEN_GROUP, :]
            gates_ref[pl.ds(t0, TOKEN_GROUP), a * nk:(a + 1) * nk] = rows.astype(BF16)
        return carry

    lax.fori_loop(0, tb // TOKEN_GROUP, group_body, 0)


def _peer_dense_kernel(xn_ref, i1_ref, i2_ref, g_ref, u_ref, v_ref, h_ref, o_ref, acc_ref, gates_ref, tile_ref):
    j = pl.program_id(1)
    eb = u_ref.shape[0]

    @pl.when(j == 0)
    def _():
        acc_ref[...] = jnp.zeros_like(acc_ref)
        _build_gates(i1_ref, i2_ref, g_ref, gates_ref, tile_ref)

    act = _gelu(_dot_nt(xn_ref[...], u_ref[...]))
    gates = gates_ref[:, pl.ds(pl.multiple_of(j * eb, eb), eb)]
    acc_ref[...] += _dot((gates.astype(F32) * act).astype(BF16), v_ref[...])

    @pl.when(j == pl.num_programs(1) - 1)
    def _():
        o_ref[...] = h_ref[...] + acc_ref[...]


def _peer_dense(xn, i1, i2, gate, u, v, h, tb, eb):
    n, d = h.shape
    n_exp = u.shape[0]
    tok = lambda w: pl.BlockSpec((tb, w), lambda i, j: (i, 0))
    return pl.pallas_call(
        _peer_dense_kernel,
        grid=(n // tb, n_exp // eb),
        in_specs=[tok(d), tok(i1.shape[1]), tok(i1.shape[1]), tok(i1.shape[1]),
                  pl.BlockSpec((eb, d), lambda i, j: (j, 0)),
                  pl.BlockSpec((eb, d), lambda i, j: (j, 0)),
                  tok(d)],
        out_specs=tok(d),
        out_shape=jax.ShapeDtypeStruct((n, d), F32),
        scratch_shapes=[pltpu.VMEM((tb, d), F32), pltpu.VMEM((tb, n_exp), BF16),
                        pltpu.VMEM((PEER_NKEYS * TILE_ROW_STRIDE, PEER_NKEYS), F32)],
        compiler_params=_cparams("parallel", "arbitrary"),
        name="peer_dense",
    )(xn, i1, i2, gate, u, v, h)


def _peer(h, g, wq, sk, u, v, tb_sel, tb, eb):
    xn, i1, i2, gate = _peer_select(h, g, wq, sk, tb_sel)
    return _peer_dense(xn, i1, i2, gate, u, v, h, tb, eb)


def _ple_kernel(h_ref, p_ref, g_ref, wg_ref, wp_ref, gf_ref, o_ref, *, final):
    h = h_ref[...]
    gate = jax.nn.sigmoid(_dot(_rms(h, g_ref[...]).astype(BF16), wg_ref[...]))
    h = h + gate * _dot(p_ref[...].astype(BF16), wp_ref[...])
    o_ref[...] = _rms(h, gf_ref[...]) if final else h


def _ple(h, p, g, wg, wp, gf, tb, final):
    n, d = h.shape
    return pl.pallas_call(
        functools.partial(_ple_kernel, final=final),
        grid=(n // tb,),
        in_specs=[pl.BlockSpec((tb, d), lambda i: (i, 0)), pl.BlockSpec((tb, p.shape[1]), lambda i: (i, 0)),
                  _full(g.shape), _full(wg.shape), _full(wp.shape), _full(gf.shape)],
        out_specs=pl.BlockSpec((tb, d), lambda i: (i, 0)),
        out_shape=jax.ShapeDtypeStruct((n, d), F32),
        compiler_params=_cparams("parallel"),
        name="ple_final" if final else "ple",
    )(h, p, g, wg, wp, gf)


def _qkv_kernel(h_ref, gm_ref, gkv_ref, wq_ref, wkv_ref, q_ref, kv_ref):
    h = h_ref[...]
    q_ref[...] = _dot(_rms(h, gm_ref[...]).astype(BF16), wq_ref[...])
    kv_ref[...] = _dot(_rms(h, gkv_ref[...]).astype(BF16), wkv_ref[...])


def _qkv(h, gm, gkv, wq, wkv, tb):
    n, d = h.shape
    return pl.pallas_call(
        _qkv_kernel,
        grid=(n // tb,),
        in_specs=[pl.BlockSpec((tb, d), lambda i: (i, 0)), _full(gm.shape), _full(gkv.shape),
                  _full(wq.shape), _full(wkv.shape)],
        out_specs=[pl.BlockSpec((tb, wq.shape[1]), lambda i: (i, 0)),
                   pl.BlockSpec((tb, wkv.shape[1]), lambda i: (i, 0))],
        out_shape=[jax.ShapeDtypeStruct((n, wq.shape[1]), F32),
                   jax.ShapeDtypeStruct((n, wkv.shape[1]), F32)],
        compiler_params=_cparams("parallel"),
        name="qkv",
    )(h, gm, gkv, wq, wkv)


def _qkv_strided_kernel(h_ref, gm_ref, gkv_ref, wq_ref, wkv_ref, *refs):
    n_br = len(DILATIONS)
    tail_ref, outs, (q_scr, kv_scr) = refs[0], refs[1:1 + 3 * n_br], refs[1 + 3 * n_br:]
    tb = h_ref.shape[0]
    hw = N_HEADS_B * HEAD_DIM
    tiles = hw // LANES
    h = h_ref[...]
    q = _dot(_rms(h, gm_ref[...]).astype(BF16), wq_ref[...])
    kv = _dot(_rms(h, gkv_ref[...]).astype(BF16), wkv_ref[...])
    tail_ref[...] = kv
    for j in range(q_scr.shape[0]):
        q_scr[j] = q[:, j * LANES:(j + 1) * LANES]
    for j in range(kv_scr.shape[0]):
        kv_scr[j] = kv[:, j * LANES:(j + 1) * LANES]
    for g, r in enumerate(DILATIONS):
        q_ref, k_ref, v_ref = outs[3 * g:3 * g + 3]
        for c in range(r):
            rows = pl.ds(c, tb // r, stride=r)
            for j in range(tiles):
                cols = slice(j * LANES, (j + 1) * LANES)
                q_ref[c, :, cols] = q_scr[g * tiles + j, rows, :].astype(BF16)
                k_ref[c, :, cols] = kv_scr[2 * g * tiles + j, rows, :].astype(BF16)
                v_ref[c, :, cols] = kv_scr[(2 * g + 1) * tiles + j, rows, :].astype(BF16)


def _qkv_strided(h, gm, gkv, wq, wkv, tb, tail_rows):
    n, d = h.shape
    hw = N_HEADS_B * HEAD_DIM
    first_tail_block = (n - tail_rows) // tb
    out_specs = [pl.BlockSpec((tb, wkv.shape[1]), lambda i: (jnp.maximum(i - first_tail_block, 0), 0))]
    out_shape = [jax.ShapeDtypeStruct((tail_rows, wkv.shape[1]), F32)]
    for r in DILATIONS:
        out_specs += [pl.BlockSpec((r, tb // r, hw), lambda i: (0, i, 0))] * 3
        out_shape += [jax.ShapeDtypeStruct((r, n // r, hw), BF16)] * 3
    outs = pl.pallas_call(
        _qkv_strided_kernel,
        grid=(n // tb,),
        in_specs=[pl.BlockSpec((tb, d), lambda i: (i, 0)), _full(gm.shape), _full(gkv.shape),
                  _full(wq.shape), _full(wkv.shape)],
        out_specs=out_specs,
        out_shape=out_shape,
        scratch_shapes=[pltpu.VMEM((wq.shape[1] // LANES, tb, LANES), F32),
                        pltpu.VMEM((wkv.shape[1] // LANES, tb, LANES), F32)],
        compiler_params=_cparams("arbitrary"),
        name="qkv_strided",
    )(h, gm, gkv, wq, wkv)
    return outs[0], [outs[1 + 3 * g:4 + 3 * g] for g in range(len(DILATIONS))]


def _alibi_slopes():
    n = len(WINDOWS) * N_HEADS_B
    e = (np.arange(n, dtype=np.float32) + 1.0) * np.float32(8.0 / n)
    return np.exp2(-e).astype(np.float32).reshape(len(WINDOWS), N_HEADS_B)


def _attn_seq_kernel(q_ref, kc_ref, kp_ref, vc_ref, vp_ref, o_ref, lse_ref, *, dilation, slopes):
    b = pl.program_id(1)
    qi = lax.broadcasted_iota(jnp.int32, (SPAN, SPAN), 0)
    ki = lax.broadcasted_iota(jnp.int32, (SPAN, SPAN), 1)
    dist_c = (qi - ki).astype(F32) * float(dilation)
    dist_p = (qi - ki + SPAN).astype(F32) * float(dilation)
    valid_c = ki <= qi
    valid_p = jnp.logical_and(ki >= qi, b > 0)
    scale = HEAD_DIM ** -0.5
    for h in range(N_HEADS_B):
        hs = slice(h * HEAD_DIM, (h + 1) * HEAD_DIM)
        qh = q_ref[:, hs]
        s_c = _dot_nt(qh, kc_ref[:, hs]) * scale - slopes[h] * dist_c
        s_p = _dot_nt(qh, kp_ref[:, hs]) * scale - slopes[h] * dist_p
        s_c = jnp.where(valid_c, s_c, NEG)
        s_p = jnp.where(valid_p, s_p, NEG)
        m = jnp.maximum(jnp.max(s_c, axis=-1, keepdims=True), jnp.max(s_p, axis=-1, keepdims=True))
        e_c = jnp.exp(s_c - m)
        e_p = jnp.exp(s_p - m)
        l = jnp.sum(e_c, axis=-1, keepdims=True) + jnp.sum(e_p, axis=-1, keepdims=True)
        o = _dot(e_c.astype(BF16), vc_ref[:, hs]) + _dot(e_p.astype(BF16), vp_ref[:, hs])
        o_ref[:, hs] = o / l
        lse_ref[:, hs] = jnp.broadcast_to(m + jnp.log(l), (SPAN, HEAD_DIM))


def _attn_seq(q, k, v, branch):
    r, length, hw = q.shape
    cur = pl.BlockSpec((None, SPAN, hw), lambda c, b: (c, b, 0))
    prev = pl.BlockSpec((None, SPAN, hw), lambda c, b: (c, jnp.maximum(b - 1, 0), 0))
    return pl.pallas_call(
        functools.partial(_attn_seq_kernel, dilation=r, slopes=[float(s) for s in _alibi_slopes()[branch]]),
        grid=(r, length // SPAN),
        in_specs=[cur, cur, prev, cur, prev],
        out_specs=[cur, cur],
        out_shape=[jax.ShapeDtypeStruct((r, length, hw), F32)] * 2,
        compiler_params=_cparams("parallel", "arbitrary"),
        name=f"attn_seq_w{WINDOWS[branch]}",
    )(q, k, k, v, v)


STEP_HEAD_GROUPS = 2


def _attn_step_kernel(q_ref, kvn_ref, sl_ref, *refs):
    n_br = len(WINDOWS)
    st_refs, new_refs, (o_ref, lse_ref) = refs[:n_br], refs[n_br:2 * n_br], refs[2 * n_br:]
    rows = q_ref.shape[-2]
    heads = rows // HEAD_DIM
    scale = HEAD_DIM ** -0.5
    per_head = lambda a: a.reshape(heads, HEAD_DIM, a.shape[-1])
    for g in range(n_br):
        w_len = st_refs[g].shape[-1]
        q = per_head(q_ref[0, g, 0])
        k_new, v_new = kvn_ref[0, g, 0, 0], kvn_ref[0, g, 1, 0]
        k_old, v_old = st_refs[g][0, 0, 0], st_refs[g][0, 1, 0]
        slope = per_head(sl_ref[g, 0])[:, 0:1, :]
        pos = lax.broadcasted_iota(jnp.int32, (heads, 1, w_len), 2)
        s_old = jnp.sum(per_head(k_old) * q, axis=1, keepdims=True) * scale - slope * (w_len - pos).astype(F32)
        s_old = jnp.where(lax.rem(pos, DILATIONS[g]) == 0, s_old, NEG)
        s_new = jnp.sum(per_head(k_new) * q, axis=1, keepdims=True) * scale
        m = jnp.maximum(jnp.max(s_old, axis=2, keepdims=True), s_new)
        e_old = jnp.exp(s_old - m)
        e_new = jnp.exp(s_new - m)
        l = jnp.sum(e_old, axis=2, keepdims=True) + e_new
        o = (jnp.sum(e_old * per_head(v_old), axis=2, keepdims=True) + e_new * per_head(v_new)) / l
        o_ref[0, g, 0] = o.reshape(rows, 1)
        lse_ref[0, g, 0] = jnp.broadcast_to(m + jnp.log(l), o.shape).reshape(rows, 1)
        last = lax.broadcasted_iota(jnp.int32, (rows, w_len), 1) == w_len - 1
        for kv, old, new in ((0, k_old, k_new), (1, v_old, v_new)):
            new_refs[g][0, kv, 0] = jnp.where(last, new, pltpu.roll(old, w_len - 1, axis=1))


def _attn_step(q, kv_new, states):
    nb, n_br = q.shape[:2]
    hg = STEP_HEAD_GROUPS
    rows = N_HEADS_B // hg * HEAD_DIM
    slopes = jnp.asarray(np.repeat(_alibi_slopes(), HEAD_DIM, axis=1).reshape(n_br, hg, rows, 1))
    st_args, st_specs = [], []
    for g in range(n_br):
        assert states[g].shape[1] == WINDOWS[g], "window buffers must be full"
        st_args.append(jnp.transpose(states[g], (0, 2, 3, 4, 1)).reshape(nb, 2, hg, rows, WINDOWS[g]))
        st_specs.append(pl.BlockSpec((1, 2, 1, rows, WINDOWS[g]), lambda n, j: (n, 0, j, 0, 0)))
    q = q.reshape(nb, n_br, hg, rows, 1)
    kv_new = kv_new.reshape(nb, n_br, 2, hg, rows, 1)
    vec = pl.BlockSpec((1, n_br, 1, rows, 1), lambda n, j: (n, 0, j, 0, 0))
    outs = pl.pallas_call(
        _attn_step_kernel,
        grid=(nb, hg),
        in_specs=[vec, pl.BlockSpec((1, n_br, 2, 1, rows, 1), lambda n, j: (n, 0, 0, j, 0, 0)),
                  pl.BlockSpec((n_br, 1, rows, 1), lambda n, j: (0, j, 0, 0))] + st_specs,
        out_specs=st_specs + [vec, vec],
        out_shape=[jax.ShapeDtypeStruct(a.shape, F32) for a in st_args] + [jax.ShapeDtypeStruct(q.shape, F32)] * 2,
        compiler_params=_cparams("parallel", "parallel"),
        name="attn_step",
    )(q, kv_new, slopes, *st_args)
    new_states = [jnp.transpose(a.reshape(nb, 2, N_HEADS_B, HEAD_DIM, WINDOWS[g]), (0, 4, 1, 2, 3))
                  for g, a in enumerate(outs[:n_br])]
    o, lse = (a.reshape(nb, n_br, N_HEADS_B * HEAD_DIM) for a in outs[n_br:])
    return o, lse, new_states


def _attn_out_kernel(*refs, dilations):
    n_br = len(dilations)
    o_refs, lse_refs = refs[:n_br], refs[n_br:2 * n_br]
    h_ref, wo_ref, out_ref = refs[2 * n_br:2 * n_br + 3]
    scratch = refs[2 * n_br + 3:]
    tb = h_ref.shape[0]

    def token_order(ref, scr_ref, r):
        if r == 1:
            return ref[0]
        tiles = scr_ref.shape[0]
        for c in range(r):
            for j in range(tiles):
                scr_ref[j, pl.ds(c, tb // r, stride=r), :] = ref[c, :, j * LANES:(j + 1) * LANES]
        return jnp.concatenate([scr_ref[j] for j in range(tiles)], axis=1)

    lses = [token_order(ref, scratch[g], dilations[g]) for g, ref in enumerate(lse_refs)]
    m = functools.reduce(jnp.maximum, lses)
    ws = [jnp.exp(l - m) for l in lses]
    tot = functools.reduce(lambda a, b: a + b, ws)
    o = functools.reduce(lambda a, b: a + b, [w / tot * token_order(ref, scratch[n_br + g], dilations[g])
                                              for g, (w, ref) in enumerate(zip(ws, o_refs))])
    out_ref[...] = h_ref[...] + _dot(o.astype(BF16), wo_ref[...])


def _attn_out(os, lses, h, wo, tb, dilations):
    n, d = h.shape
    hw = wo.shape[0]
    specs = [pl.BlockSpec((r, tb // r, hw), lambda i: (0, i, 0)) for r in dilations]
    return pl.pallas_call(
        functools.partial(_attn_out_kernel, dilations=dilations),
        grid=(n // tb,),
        in_specs=specs + specs + [pl.BlockSpec((tb, d), lambda i: (i, 0)), _full(wo.shape)],
        out_specs=pl.BlockSpec((tb, d), lambda i: (i, 0)),
        out_shape=jax.ShapeDtypeStruct((n, d), F32),
        scratch_shapes=[pltpu.VMEM((hw // LANES, tb, LANES), F32)] * (2 * len(dilations)),
        compiler_params=_cparams("parallel"),
        name="attn_out",
    )(*os, *lses, h, wo)


def kernel(x_prompt, x_sample, state_kv_w128, state_kv_w512, state_kv_w2048, p_prompt, p_sample, g_mix, sgu_w_in, sgu_g_v, sgu_w_s, sgu_b_s, sgu_w_out, kv_g, w_kv, attn_w_q, attn_w_o, g_ffn, peer_w_query, peer_subkeys, peer_u, peer_v, ple_g, ple_w_gate, ple_w_proj, g_final):
    depth, d = g_mix.shape
    assert depth == 2 and sgu_w_in.shape[0] == 1 and attn_w_q.shape[0] == 1
    b_p, s_p, _ = x_prompt.shape
    b_s, t_s, _ = x_sample.shape
    assert b_p == 1 and t_s == 1 and s_p % max(WINDOWS) == 0
    states = (state_kv_w128, state_kv_w512, state_kv_w2048)
    hw = N_HEADS_B * HEAD_DIM
    n_br = len(WINDOWS)

    row = lambda a: a.reshape(1, -1).astype(F32)
    bf = lambda a: a.astype(BF16)

    w_in, w_out = bf(sgu_w_in[0]), bf(sgu_w_out[0])
    gw = d // SGU_GROUPS
    bs_full = jnp.repeat(sgu_b_s[0].T, gw, axis=1)
    ws0 = jnp.repeat(sgu_w_s[0][:, 0, 0], gw).reshape(1, d)
    bs0 = bs_full[0:1]
    wq_peer = bf(peer_w_query)
    subkeys = bf(peer_subkeys.reshape(depth, 2 * PEER_HEADS, PEER_NKEYS, -1))
    u_tab = bf(peer_u)
    v_tab = bf(peer_v)
    w_gate, w_proj = bf(ple_w_gate), bf(ple_w_proj)
    w_q, w_kvb, w_o = bf(attn_w_q[0]), bf(w_kv), bf(attn_w_o[0])

    n_s = LANES
    xs = jnp.pad(x_sample.reshape(b_s, d), ((0, n_s - b_s), (0, 0)))
    ps = jnp.pad(p_sample.reshape(depth, b_s, -1), ((0, 0), (0, n_s - b_s), (0, 0)))
    xp = x_prompt.reshape(s_p, d)
    pp = p_prompt.reshape(depth, s_p, -1)

    def channel(h, p, i, tb_sel, tb, tbp, final):
        h = _peer(h, row(g_ffn[i]), wq_peer[i], subkeys[i], u_tab[i], v_tab[i], tb_sel, tb, 1024)
        return _ple(h, p[i], row(ple_g[i]), w_gate[i], w_proj[i], row(g_final), tbp, final)

    hp = _sgu_seq(xp, row(g_mix[0]), w_in, row(sgu_g_v[0]), sgu_w_s[0], bs_full, w_out, 512)
    hs, v_s = _sgu_first(xs, row(g_mix[0]), w_in, row(sgu_g_v[0]), ws0, bs0, w_out)
    hp = channel(hp, pp, 0, 256, 512, 512, False)
    hs = channel(hs, ps, 0, n_s, n_s, n_s, False)

    tail = min(max(WINDOWS), s_p)
    kv_tail, qkv_p = _qkv_strided(hp, row(g_mix[1]), row(kv_g), w_q, w_kvb, 512, tail)
    qs, kvs = _qkv(hs, row(g_mix[1]), row(kv_g), w_q, w_kvb, n_s)
    outs = [_attn_seq(*qkv_p[g], g) for g in range(n_br)]
    hp = _attn_out([o for o, _ in outs], [l for _, l in outs], hp, w_o, 512, DILATIONS)
    q_step = qs[:b_s].reshape(b_s, n_br, N_HEADS_B, HEAD_DIM)
    kv_step = kvs[:b_s].reshape(b_s, n_br, 2, N_HEADS_B, HEAD_DIM)
    o_s, lse_s, new_s = _attn_step(q_step, kv_step, states)
    pad = lambda a, g: jnp.pad(a[:, g].reshape(1, b_s, hw), ((0, 0), (0, n_s - b_s), (0, 0)))
    hs = _attn_out([pad(o_s, g) for g in range(n_br)], [pad(lse_s, g) for g in range(n_br)],
                   hs, w_o, n_s, (1,) * n_br)
    yp = channel(hp, pp, 1, 256, 512, 512, True)
    ys = channel(hs, ps, 1, n_s, n_s, n_s, True)

    kv_p = kv_tail.reshape(1, tail, n_br, 2, N_HEADS_B, HEAD_DIM)
    new_p = [kv_p[:, tail - min(WINDOWS[g], s_p):, g] for g in range(n_br)]
    return (yp.reshape(1, s_p, d), ys[:b_s].reshape(b_s, 1, d), v_s[:b_s].reshape(1, b_s, 1, d),
            new_p[0], new_p[1], new_p[2], new_s[0], new_s[1], new_s[2])
```

```python
import functools
import math

import numpy as np
import jax
import jax.numpy as jnp
from jax import lax
from jax.experimental import pallas as pl
from jax.experimental.pallas import tpu as pltpu

F32 = jnp.float32
BF16 = jnp.bfloat16

EPS = 1e-6
NEG = -1e30
CHUNK = 128
SGU_GROUPS = 8
HEAD_DIM = 64
N_HEADS_B = 8
WINDOWS = (128, 512, 2048)
DILATIONS = (1, 4, 16)
SPAN = 128
PEER_HEADS = 8
PEER_NKEYS = 128
PEER_TOPK = 16

LANES = 128
VMEM_LIMIT = 56 * 1024 * 1024


def _cparams(*sem):
    return pltpu.CompilerParams(dimension_semantics=sem, vmem_limit_bytes=VMEM_LIMIT)


def _rms(x, g):
    return x * lax.rsqrt(jnp.mean(x * x, axis=-1, keepdims=True) + EPS) * g


def _gelu(x):
    return 0.5 * x * (1.0 + lax.erf(x * (1.0 / math.sqrt(2.0))))


def _dot(a, b):
    return jnp.dot(a, b, preferred_element_type=F32)


def _dot_nt(a, b):
    return lax.dot_general(a, b, (((1,), (1,)), ((), ())), preferred_element_type=F32)


def _full(shape):
    nd = len(shape)
    return pl.BlockSpec(shape, lambda *_: (0,) * nd)


def _sgu_front(x, gmix_ref, win_ref, gv_ref):
    d = x.shape[-1]
    xn = _rms(x, gmix_ref[...]).astype(BF16)
    uv = _gelu(_dot(xn, win_ref[...]))
    return uv[:, :d], _rms(uv[:, d:], gv_ref[...])


def _sgu_seq_kernel(x_ref, gmix_ref, win_ref, gv_ref, ws_ref, bs_ref, wout_ref, h_ref, mixed_ref):
    x = x_ref[...]
    u, v = _sgu_front(x, gmix_ref, win_ref, gv_ref)
    vb = v.astype(BF16)
    gw = v.shape[-1] // SGU_GROUPS
    row = lax.broadcasted_iota(jnp.int32, (CHUNK, CHUNK), 0)
    col = lax.broadcasted_iota(jnp.int32, (CHUNK, CHUNK), 1)
    for g in range(SGU_GROUPS):
        wsg = jnp.where(col <= row, ws_ref[g], 0.0).astype(BF16)
        for c in range(x.shape[0] // CHUNK):
            rs = slice(c * CHUNK, (c + 1) * CHUNK)
            cs = slice(g * gw, (g + 1) * gw)
            mixed_ref[rs, cs] = _dot(wsg, vb[rs, cs]) + bs_ref[:, cs]
    z = (u * mixed_ref[...]).astype(BF16)
    h_ref[...] = x + _dot(z, wout_ref[...])


def _sgu_first_kernel(x_ref, gmix_ref, win_ref, gv_ref, ws0_ref, bs0_ref, wout_ref, h_ref, v_ref):
    x = x_ref[...]
    u, v = _sgu_front(x, gmix_ref, win_ref, gv_ref)
    v_ref[...] = v
    z = (u * (v * ws0_ref[...] + bs0_ref[...])).astype(BF16)
    h_ref[...] = x + _dot(z, wout_ref[...])


def _sgu_seq(x, gmix, win, gv, ws, bs_full, wout, tb):
    n, d = x.shape
    return pl.pallas_call(
        _sgu_seq_kernel,
        grid=(n // tb,),
        in_specs=[pl.BlockSpec((tb, d), lambda i: (i, 0)), _full(gmix.shape), _full(win.shape),
                  _full(gv.shape), _full(ws.shape), _full(bs_full.shape), _full(wout.shape)],
        out_specs=pl.BlockSpec((tb, d), lambda i: (i, 0)),
        out_shape=jax.ShapeDtypeStruct((n, d), F32),
        scratch_shapes=[pltpu.VMEM((tb, d), F32)],
        compiler_params=_cparams("parallel"),
        name="sgu_seq",
    )(x, gmix, win, gv, ws, bs_full, wout)


def _sgu_first(x, gmix, win, gv, ws0, bs0, wout):
    n, d = x.shape
    return pl.pallas_call(
        _sgu_first_kernel,
        grid=(1,),
        in_specs=[_full(x.shape), _full(gmix.shape), _full(win.shape), _full(gv.shape),
                  _full(ws0.shape), _full(bs0.shape), _full(wout.shape)],
        out_specs=[_full((n, d)), _full((n, d))],
        out_shape=[jax.ShapeDtypeStruct((n, d), F32), jax.ShapeDtypeStruct((n, d), F32)],
        compiler_params=_cparams("arbitrary"),
        name="sgu_first",
    )(x, gmix, win, gv, ws0, bs0, wout)


_CAND_COUNTS = [PEER_TOPK // (i + 1) for i in range(PEER_TOPK)]
TOKEN_GROUP = 16
TILE_ROW_STRIDE = 24


def _insert_top(entries, payloads):
    shape = entries[0].shape
    vals = [jnp.full(shape, -jnp.inf, F32)] * PEER_TOPK
    pays = [[jnp.zeros(shape, F32)] * PEER_TOPK for _ in payloads]
    for e, x in enumerate(entries):
        above = [x > v for v in vals]

        def shifted(old, new, j):
            if j == 0:
                return jnp.where(above[0], new, old[0])
            return jnp.where(above[j], jnp.where(above[j - 1], old[j - 1], new), old[j])

        pays = [[shifted(old, p[e], j) for j in range(PEER_TOPK)] for old, p in zip(pays, payloads)]
        vals = [shifted(vals, x, j) for j in range(PEER_TOPK)]
    return vals, pays


def _peer_select_kernel(h_ref, g_ref, wq_ref, sk_ref, xn_ref, i1_ref, i2_ref, gate_ref,
                        q_scr, sc1_scr, sc2_scr, a_scr, b_scr, g_scr):
    tb = h_ref.shape[0]
    chunks = tb // LANES
    xn = _rms(h_ref[...], g_ref[...]).astype(BF16)
    xn_ref[...] = xn
    q_scr[...] = _dot(xn, wq_ref[...])
    dk = sk_ref.shape[-1]

    def head_body(h, carry):
        halves = []
        for sc_scr, hc in ((sc1_scr, 2 * h), (sc2_scr, 2 * h + 1)):
            cols = pl.ds(pl.multiple_of(hc * dk, dk), dk)
            for s in range(chunks):
                qc = q_scr[s * LANES:(s + 1) * LANES, cols].astype(BF16)
                sc_scr[pl.ds(s, PEER_NKEYS, stride=chunks), :] = _dot_nt(sk_ref[hc], qc)
            keys = [sc_scr[k * chunks:(k + 1) * chunks, :] for k in range(PEER_NKEYS)]
            vals, (idx,) = _insert_top(keys, [[float(k) for k in range(PEER_NKEYS)]])
            halves.append((vals, idx))
        (t1, k1), (t2, k2) = halves
        pairs = [(i, j) for i, nj in enumerate(_CAND_COUNTS) for j in range(nj)]
        top, (ka, kb) = _insert_top([t1[i] + t2[j] for i, j in pairs],
                                    [[k1[i] for i, _ in pairs], [k2[j] for _, j in pairs]])
        e = [jnp.exp(t - top[0]) for t in top]
        total = functools.reduce(lambda x, y: x + y, e)
        stack = lambda tiles: jnp.concatenate(tiles, axis=0)
        a_scr[h] = stack(ka)
        b_scr[h] = stack(kb)
        g_scr[h] = stack([x / total for x in e])
        return carry

    lax.fori_loop(0, PEER_HEADS, head_body, 0)
    for out_ref, scr in ((i1_ref, a_scr), (i2_ref, b_scr), (gate_ref, g_scr)):
        for s in range(chunks):
            sel = jnp.concatenate([scr[h, pl.ds(s, PEER_TOPK, stride=chunks), :] for h in range(PEER_HEADS)], axis=0)
            out_ref[s * LANES:(s + 1) * LANES, :] = sel.T


def _peer_select(h, g, wq, sk, tb):
    n, d = h.shape
    nhc, nk, dk = sk.shape
    n_sel = PEER_HEADS * PEER_TOPK
    chunks = tb // LANES
    sel = pl.BlockSpec((tb, n_sel), lambda i: (i, 0))
    return pl.pallas_call(
        _peer_select_kernel,
        grid=(n // tb,),
        in_specs=[pl.BlockSpec((tb, d), lambda i: (i, 0)), _full(g.shape), _full(wq.shape), _full(sk.shape)],
        out_specs=[pl.BlockSpec((tb, d), lambda i: (i, 0)), sel, sel, sel],
        out_shape=[jax.ShapeDtypeStruct((n, d), BF16)] + [jax.ShapeDtypeStruct((n, n_sel), F32)] * 3,
        scratch_shapes=[pltpu.VMEM((tb, wq.shape[1]), F32),
                        pltpu.VMEM((nk * chunks, LANES), F32),
                        pltpu.VMEM((nk * chunks, LANES), F32)]
                       + [pltpu.VMEM((PEER_HEADS, PEER_TOPK * chunks, LANES), F32)] * 3,
        compiler_params=_cparams("parallel"),
        name="peer_select",
    )(h, g, wq, sk)


def _build_gates(i1_ref, i2_ref, g_ref, gates_ref, tile_ref):
    tb = i1_ref.shape[0]
    nk = PEER_NKEYS
    sub = lax.broadcasted_iota(jnp.int32, (nk, i1_ref.shape[1]), 0).astype(F32)

    def group_body(gi, carry):
        t0 = pl.multiple_of(gi * TOKEN_GROUP, TOKEN_GROUP)
        for tt in range(TOKEN_GROUP):
            row = pl.ds(t0 + tt, 1)
            first = jnp.where(sub == i1_ref[row, :], g_ref[row, :], 0.0).astype(BF16)
            second = jnp.where(sub == i2_ref[row, :], 1.0, 0.0).astype(BF16)
            tile_ref[pl.ds(tt, nk, stride=TILE_ROW_STRIDE), :] = _dot_nt(first, second)
        for a in range(nk):
            rows = tile_ref[a * TILE_ROW_STRIDE:a * TILE_ROW_STRIDE + TOKEN_GROUP, :]
            gates_ref[pl.ds(t0, TOKEN_GROUP), a * nk:(a + 1) * nk] = rows.astype(BF16)
        return carry

    lax.fori_loop(0, tb // TOKEN_GROUP, group_body, 0)


def _peer_dense_kernel(xn_ref, i1_ref, i2_ref, g_ref, u_ref, v_ref, h_ref, o_ref, acc_ref, gates_ref, tile_ref):
    j = pl.program_id(1)
    eb = u_ref.shape[0]

    @pl.when(j == 0)
    def _():
        acc_ref[...] = jnp.zeros_like(acc_ref)
        _build_gates(i1_ref, i2_ref, g_ref, gates_ref, tile_ref)

    act = _gelu(_dot_nt(xn_ref[...], u_ref[...]))
    gates = gates_ref[:, pl.ds(pl.multiple_of(j * eb, eb), eb)]
    acc_ref[...] += _dot((gates.astype(F32) * act).astype(BF16), v_ref[...])

    @pl.when(j == pl.num_programs(1) - 1)
    def _():
        o_ref[...] = h_ref[...] + acc_ref[...]


def _peer_dense(xn, i1, i2, gate, u, v, h, tb, eb):
    n, d = h.shape
    n_exp = u.shape[0]
    tok = lambda w: pl.BlockSpec((tb, w), lambda i, j: (i, 0))
    return pl.pallas_call(
        _peer_dense_kernel,
        grid=(n // tb, n_exp // eb),
        in_specs=[tok(d), tok(i1.shape[1]), tok(i1.shape[1]), tok(i1.shape[1]),
                  pl.BlockSpec((eb, d), lambda i, j: (j, 0)),
                  pl.BlockSpec((eb, d), lambda i, j: (j, 0)),
                  tok(d)],
        out_specs=tok(d),
        out_shape=jax.ShapeDtypeStruct((n, d), F32),
        scratch_shapes=[pltpu.VMEM((tb, d), F32), pltpu.VMEM((tb, n_exp), BF16),
                        pltpu.VMEM((PEER_NKEYS * TILE_ROW_STRIDE, PEER_NKEYS), F32)],
        compiler_params=_cparams("parallel", "arbitrary"),
        name="peer_dense",
    )(xn, i1, i2, gate, u, v, h)


def _peer(h, g, wq, sk, u, v, tb_sel, tb, eb):
    xn, i1, i2, gate = _peer_select(h, g, wq, sk, tb_sel)
    return _peer_dense(xn, i1, i2, gate, u, v, h, tb, eb)


def _ple_kernel(h_ref, p_ref, g_ref, wg_ref, wp_ref, gf_ref, o_ref, *, final):
    h = h_ref[...]
    gate = jax.nn.sigmoid(_dot(_rms(h, g_ref[...]).astype(BF16), wg_ref[...]))
    h = h + gate * _dot(p_ref[...].astype(BF16), wp_ref[...])
    o_ref[...] = _rms(h, gf_ref[...]) if final else h


def _ple(h, p, g, wg, wp, gf, tb, final):
    n, d = h.shape
    return pl.pallas_call(
        functools.partial(_ple_kernel, final=final),
        grid=(n // tb,),
        in_specs=[pl.BlockSpec((tb, d), lambda i: (i, 0)), pl.BlockSpec((tb, p.shape[1]), lambda i: (i, 0)),
                  _full(g.shape), _full(wg.shape), _full(wp.shape), _full(gf.shape)],
        out_specs=pl.BlockSpec((tb, d), lambda i: (i, 0)),
        out_shape=jax.ShapeDtypeStruct((n, d), F32),
        compiler_params=_cparams("parallel"),
        name="ple_final" if final else "ple",
    )(h, p, g, wg, wp, gf)


def _qkv_kernel(h_ref, gm_ref, gkv_ref, wq_ref, wkv_ref, q_ref, kv_ref):
    h = h_ref[...]
    q_ref[...] = _dot(_rms(h, gm_ref[...]).astype(BF16), wq_ref[...])
    kv_ref[...] = _dot(_rms(h, gkv_ref[...]).astype(BF16), wkv_ref[...])


def _qkv(h, gm, gkv, wq, wkv, tb):
    n, d = h.shape
    return pl.pallas_call(
        _qkv_kernel,
        grid=(n // tb,),
        in_specs=[pl.BlockSpec((tb, d), lambda i: (i, 0)), _full(gm.shape), _full(gkv.shape),
                  _full(wq.shape), _full(wkv.shape)],
        out_specs=[pl.BlockSpec((tb, wq.shape[1]), lambda i: (i, 0)),
                   pl.BlockSpec((tb, wkv.shape[1]), lambda i: (i, 0))],
        out_shape=[jax.ShapeDtypeStruct((n, wq.shape[1]), F32),
                   jax.ShapeDtypeStruct((n, wkv.shape[1]), F32)],
        compiler_params=_cparams("parallel"),
        name="qkv",
    )(h, gm, gkv, wq, wkv)


def _qkv_strided_kernel(h_ref, gm_ref, gkv_ref, wq_ref, wkv_ref, *refs):
    n_br = len(DILATIONS)
    tail_ref, outs, (q_scr, kv_scr) = refs[0], refs[1:1 + 3 * n_br], refs[1 + 3 * n_br:]
    tb = h_ref.shape[0]
    hw = N_HEADS_B * HEAD_DIM
    tiles = hw // LANES
    h = h_ref[...]
    q = _dot(_rms(h, gm_ref[...]).astype(BF16), wq_ref[...])
    kv = _dot(_rms(h, gkv_ref[...]).astype(BF16), wkv_ref[...])
    tail_ref[...] = kv
    for j in range(q_scr.shape[0]):
        q_scr[j] = q[:, j * LANES:(j + 1) * LANES]
    for j in range(kv_scr.shape[0]):
        kv_scr[j] = kv[:, j * LANES:(j + 1) * LANES]
    for g, r in enumerate(DILATIONS):
        q_ref, k_ref, v_ref = outs[3 * g:3 * g + 3]
        for c in range(r):
            rows = pl.ds(c, tb // r, stride=r)
            for j in range(tiles):
                cols = slice(j * LANES, (j + 1) * LANES)
                q_ref[c, :, cols] = q_scr[g * tiles + j, rows, :].astype(BF16)
                k_ref[c, :, cols] = kv_scr[2 * g * tiles + j, rows, :].astype(BF16)
                v_ref[c, :, cols] = kv_scr[(2 * g + 1) * tiles + j, rows, :].astype(BF16)


def _qkv_strided(h, gm, gkv, wq, wkv, tb, tail_rows):
    n, d = h.shape
    hw = N_HEADS_B * HEAD_DIM
    first_tail_block = (n - tail_rows) // tb
    out_specs = [pl.BlockSpec((tb, wkv.shape[1]), lambda i: (jnp.maximum(i - first_tail_block, 0), 0))]
    out_shape = [jax.ShapeDtypeStruct((tail_rows, wkv.shape[1]), F32)]
    for r in DILATIONS:
        out_specs += [pl.BlockSpec((r, tb // r, hw), lambda i: (0, i, 0))] * 3
        out_shape += [jax.ShapeDtypeStruct((r, n // r, hw), BF16)] * 3
    outs = pl.pallas_call(
        _qkv_strided_kernel,
        grid=(n // tb,),
        in_specs=[pl.BlockSpec((tb, d), lambda i: (i, 0)), _full(gm.shape), _full(gkv.shape),
                  _full(wq.shape), _full(wkv.shape)],
        out_specs=out_specs,
        out_shape=out_shape,
        scratch_shapes=[pltpu.VMEM((wq.shape[1] // LANES, tb, LANES), F32),
                        pltpu.VMEM((wkv.shape[1] // LANES, tb, LANES), F32)],
        compiler_params=_cparams("arbitrary"),
        name="qkv_strided",
    )(h, gm, gkv, wq, wkv)
    return outs[0], [outs[1 + 3 * g:4 + 3 * g] for g in range(len(DILATIONS))]


def _alibi_slopes():
    n = len(WINDOWS) * N_HEADS_B
    e = (np.arange(n, dtype=np.float32) + 1.0) * np.float32(8.0 / n)
    return np.exp2(-e).astype(np.float32).reshape(len(WINDOWS), N_HEADS_B)


def _attn_seq_kernel(q_ref, kc_ref, kp_ref, vc_ref, vp_ref, o_ref, lse_ref, *, dilation, slopes):
    b = pl.program_id(1)
    qi = lax.broadcasted_iota(jnp.int32, (SPAN, SPAN), 0)
    ki = lax.broadcasted_iota(jnp.int32, (SPAN, SPAN), 1)
    dist_c = (qi - ki).astype(F32) * float(dilation)
    dist_p = (qi - ki + SPAN).astype(F32) * float(dilation)
    valid_c = ki <= qi
    valid_p = jnp.logical_and(ki >= qi, b > 0)
    scale = HEAD_DIM ** -0.5
    for h in range(N_HEADS_B):
        hs = slice(h * HEAD_DIM, (h + 1) * HEAD_DIM)
        qh = q_ref[:, hs]
        s_c = _dot_nt(qh, kc_ref[:, hs]) * scale - slopes[h] * dist_c
        s_p = _dot_nt(qh, kp_ref[:, hs]) * scale - slopes[h] * dist_p
        s_c = jnp.where(valid_c, s_c, NEG)
        s_p = jnp.where(valid_p, s_p, NEG)
        m = jnp.maximum(jnp.max(s_c, axis=-1, keepdims=True), jnp.max(s_p, axis=-1, keepdims=True))
        e_c = jnp.exp(s_c - m)
        e_p = jnp.exp(s_p - m)
        l = jnp.sum(e_c, axis=-1, keepdims=True) + jnp.sum(e_p, axis=-1, keepdims=True)
        o = _dot(e_c.astype(BF16), vc_ref[:, hs]) + _dot(e_p.astype(BF16), vp_ref[:, hs])
        o_ref[:, hs] = o / l
        lse_ref[:, hs] = jnp.broadcast_to(m + jnp.log(l), (SPAN, HEAD_DIM))


def _attn_seq(q, k, v, branch):
    r, length, hw = q.shape
    cur = pl.BlockSpec((None, SPAN, hw), lambda c, b: (c, b, 0))
    prev = pl.BlockSpec((None, SPAN, hw), lambda c, b: (c, jnp.maximum(b - 1, 0), 0))
    return pl.pallas_call(
        functools.partial(_attn_seq_kernel, dilation=r, slopes=[float(s) for s in _alibi_slopes()[branch]]),
        grid=(r, length // SPAN),
        in_specs=[cur, cur, prev, cur, prev],
        out_specs=[cur, cur],
        out_shape=[jax.ShapeDtypeStruct((r, length, hw), F32)] * 2,
        compiler_params=_cparams("parallel", "arbitrary"),
        name=f"attn_seq_w{WINDOWS[branch]}",
    )(q, k, k, v, v)


STEP_HEAD_GROUPS = 2


def _attn_step_kernel(q_ref, kvn_ref, sl_ref, *refs):
    n_br = len(WINDOWS)
    st_refs, new_refs, (o_ref, lse_ref) = refs[:n_br], refs[n_br:2 * n_br], refs[2 * n_br:]
    rows = q_ref.shape[-2]
    heads = rows // HEAD_DIM
    scale = HEAD_DIM ** -0.5
    per_head = lambda a: a.reshape(heads, HEAD_DIM, a.shape[-1])
    for g in range(n_br):
        w_len = st_refs[g].shape[-1]
        q = per_head(q_ref[0, g, 0])
        k_new, v_new = kvn_ref[0, g, 0, 0], kvn_ref[0, g, 1, 0]
        k_old, v_old = st_refs[g][0, 0, 0], st_refs[g][0, 1, 0]
        slope = per_head(sl_ref[g, 0])[:, 0:1, :]
        pos = lax.broadcasted_iota(jnp.int32, (heads, 1, w_len), 2)
        s_old = jnp.sum(per_head(k_old) * q, axis=1, keepdims=True) * scale - slope * (w_len - pos).astype(F32)
        s_old = jnp.where(lax.rem(pos, DILATIONS[g]) == 0, s_old, NEG)
        s_new = jnp.sum(per_head(k_new) * q, axis=1, keepdims=True) * scale
        m = jnp.maximum(jnp.max(s_old, axis=2, keepdims=True), s_new)
        e_old = jnp.exp(s_old - m)
        e_new = jnp.exp(s_new - m)
        l = jnp.sum(e_old, axis=2, keepdims=True) + e_new
        o = (jnp.sum(e_old * per_head(v_old), axis=2, keepdims=True) + e_new * per_head(v_new)) / l
        o_ref[0, g, 0] = o.reshape(rows, 1)
        lse_ref[0, g, 0] = jnp.broadcast_to(m + jnp.log(l), o.shape).reshape(rows, 1)
        last = lax.broadcasted_iota(jnp.int32, (rows, w_len), 1) == w_len - 1
        for kv, old, new in ((0, k_old, k_new), (1, v_old, v_new)):
            new_refs[g][0, kv, 0] = jnp.where(last, new, pltpu.roll(old, w_len - 1, axis=1))


def _attn_step(q, kv_new, states):
    nb, n_br = q.shape[:2]
    hg = STEP_HEAD_GROUPS
    rows = N_HEADS_B // hg * HEAD_DIM
    slopes = jnp.asarray(np.repeat(_alibi_slopes(), HEAD_DIM, axis=1).reshape(n_br, hg, rows, 1))
    st_args, st_specs = [], []
    for g in range(n_br):
        assert states[g].shape[1] == WINDOWS[g], "window buffers must be full"
        st_args.append(jnp.transpose(states[g], (0, 2, 3, 4, 1)).reshape(nb, 2, hg, rows, WINDOWS[g]))
        st_specs.append(pl.BlockSpec((1, 2, 1, rows, WINDOWS[g]), lambda n, j: (n, 0, j, 0, 0)))
    q = q.reshape(nb, n_br, hg, rows, 1)
    kv_new = kv_new.reshape(nb, n_br, 2, hg, rows, 1)
    vec = pl.BlockSpec((1, n_br, 1, rows, 1), lambda n, j: (n, 0, j, 0, 0))
    outs = pl.pallas_call(
        _attn_step_kernel,
        grid=(nb, hg),
        in_specs=[vec, pl.BlockSpec((1, n_br, 2, 1, rows, 1), lambda n, j: (n, 0, 0, j, 0, 0)),
                  pl.BlockSpec((n_br, 1, rows, 1), lambda n, j: (0, j, 0, 0))] + st_specs,
        out_specs=st_specs + [vec, vec],
        out_shape=[jax.ShapeDtypeStruct(a.shape, F32) for a in st_args] + [jax.ShapeDtypeStruct(q.shape, F32)] * 2,
        compiler_params=_cparams("parallel", "parallel"),
        name="attn_step",
    )(q, kv_new, slopes, *st_args)
    new_states = [jnp.transpose(a.reshape(nb, 2, N_HEADS_B, HEAD_DIM, WINDOWS[g]), (0, 4, 1, 2, 3))
                  for g, a in enumerate(outs[:n_br])]
    o, lse = (a.reshape(nb, n_br, N_HEADS_B * HEAD_DIM) for a in outs[n_br:])
    return o, lse, new_states


def _attn_out_kernel(*refs, dilations):
    n_br = len(dilations)
    o_refs, lse_refs = refs[:n_br], refs[n_br:2 * n_br]
    h_ref, wo_ref, out_ref = refs[2 * n_br:2 * n_br + 3]
    scratch = refs[2 * n_br + 3:]
    tb = h_ref.shape[0]

    def token_order(ref, scr_ref, r):
        if r == 1:
            return ref[0]
        tiles = scr_ref.shape[0]
        for c in range(r):
            for j in range(tiles):
                scr_ref[j, pl.ds(c, tb // r, stride=r), :] = ref[c, :, j * LANES:(j + 1) * LANES]
        return jnp.concatenate([scr_ref[j] for j in range(tiles)], axis=1)

    lses = [token_order(ref, scratch[g], dilations[g]) for g, ref in enumerate(lse_refs)]
    m = functools.reduce(jnp.maximum, lses)
    ws = [jnp.exp(l - m) for l in lses]
    tot = functools.reduce(lambda a, b: a + b, ws)
    o = functools.reduce(lambda a, b: a + b, [w / tot * token_order(ref, scratch[n_br + g], dilations[g])
                                              for g, (w, ref) in enumerate(zip(ws, o_refs))])
    out_ref[...] = h_ref[...] + _dot(o.astype(BF16), wo_ref[...])


def _attn_out(os, lses, h, wo, tb, dilations):
    n, d = h.shape
    hw = wo.shape[0]
    specs = [pl.BlockSpec((r, tb // r, hw), lambda i: (0, i, 0)) for r in dilations]
    return pl.pallas_call(
        functools.partial(_attn_out_kernel, dilations=dilations),
        grid=(n // tb,),
        in_specs=specs + specs + [pl.BlockSpec((tb, d), lambda i: (i, 0)), _full(wo.shape)],
        out_specs=pl.BlockSpec((tb, d), lambda i: (i, 0)),
        out_shape=jax.ShapeDtypeStruct((n, d), F32),
        scratch_shapes=[pltpu.VMEM((hw // LANES, tb, LANES), F32)] * (2 * len(dilations)),
        compiler_params=_cparams("parallel"),
        name="attn_out",
    )(*os, *lses, h, wo)


def kernel(x_prompt, x_sample, state_kv_w128, state_kv_w512, state_kv_w2048, p_prompt, p_sample, g_mix, sgu_w_in, sgu_g_v, sgu_w_s, sgu_b_s, sgu_w_out, kv_g, w_kv, attn_w_q, attn_w_o, g_ffn, peer_w_query, peer_subkeys, peer_u, peer_v, ple_g, ple_w_gate, ple_w_proj, g_final):
    depth, d = g_mix.shape
    assert depth == 2 and sgu_w_in.shape[0] == 1 and attn_w_q.shape[0] == 1
    b_p, s_p, _ = x_prompt.shape
    b_s, t_s, _ = x_sample.shape
    assert b_p == 1 and t_s == 1 and s_p % max(WINDOWS) == 0
    states = (state_kv_w128, state_kv_w512, state_kv_w2048)
    hw = N_HEADS_B * HEAD_DIM
    n_br = len(WINDOWS)

    row = lambda a: a.reshape(1, -1).astype(F32)
    bf = lambda a: a.astype(BF16)

    w_in, w_out = bf(sgu_w_in[0]), bf(sgu_w_out[0])
    gw = d // SGU_GROUPS
    bs_full = jnp.repeat(sgu_b_s[0].T, gw, axis=1)
    ws0 = jnp.repeat(sgu_w_s[0][:, 0, 0], gw).reshape(1, d)
    bs0 = bs_full[0:1]
    wq_peer = bf(peer_w_query)
    subkeys = bf(peer_subkeys.reshape(depth, 2 * PEER_HEADS, PEER_NKEYS, -1))
    u_tab = bf(peer_u)
    v_tab = bf(peer_v)
    w_gate, w_proj = bf(ple_w_gate), bf(ple_w_proj)
    w_q, w_kvb, w_o = bf(attn_w_q[0]), bf(w_kv), bf(attn_w_o[0])

    n_s = LANES
    xs = jnp.pad(x_sample.reshape(b_s, d), ((0, n_s - b_s), (0, 0)))
    ps = jnp.pad(p_sample.reshape(depth, b_s, -1), ((0, 0), (0, n_s - b_s), (0, 0)))
    xp = x_prompt.reshape(s_p, d)
    pp = p_prompt.reshape(depth, s_p, -1)

    def channel(h, p, i, tb_sel, tb, tbp, final):
        h = _peer(h, row(g_ffn[i]), wq_peer[i], subkeys[i], u_tab[i], v_tab[i], tb_sel, tb, 1024)
        return _ple(h, p[i], row(ple_g[i]), w_gate[i], w_proj[i], row(g_final), tbp, final)

    hp = _sgu_seq(xp, row(g_mix[0]), w_in, row(sgu_g_v[0]), sgu_w_s[0], bs_full, w_out, 512)
    hs, v_s = _sgu_first(xs, row(g_mix[0]), w_in, row(sgu_g_v[0]), ws0, bs0, w_out)
    hp = channel(hp, pp, 0, 1024, 512, 512, False)
    hs = channel(hs, ps, 0, n_s, n_s, n_s, False)

    tail = min(max(WINDOWS), s_p)
    kv_tail, qkv_p = _qkv_strided(hp, row(g_mix[1]), row(kv_g), w_q, w_kvb, 512, tail)
    qs, kvs = _qkv(hs, row(g_mix[1]), row(kv_g), w_q, w_kvb, n_s)
    outs = [_attn_seq(*qkv_p[g], g) for g in range(n_br)]
    hp = _attn_out([o for o, _ in outs], [l for _, l in outs], hp, w_o, 512, DILATIONS)
    q_step = qs[:b_s].reshape(b_s, n_br, N_HEADS_B, HEAD_DIM)
    kv_step = kvs[:b_s].reshape(b_s, n_br, 2, N_HEADS_B, HEAD_DIM)
    o_s, lse_s, new_s = _attn_step(q_step, kv_step, states)
    pad = lambda a, g: jnp.pad(a[:, g].reshape(1, b_s, hw), ((0, 0), (0, n_s - b_s), (0, 0)))
    hs = _attn_out([pad(o_s, g) for g in range(n_br)], [pad(lse_s, g) for g in range(n_br)],
                   hs, w_o, n_s, (1,) * n_br)
    yp = channel(hp, pp, 1, 1024, 512, 512, True)
    ys = channel(hs, ps, 1, n_s, n_s, n_s, True)

    kv_p = kv_tail.reshape(1, tail, n_br, 2, N_HEADS_B, HEAD_DIM)
    new_p = [kv_p[:, tail - min(WINDOWS[g], s_p):, g] for g in range(n_br)]
    return (yp.reshape(1, s_p, d), ys[:b_s].reshape(b_s, 1, d), v_s[:b_s].reshape(1, b_s, 1, d),
            new_p[0], new_p[1], new_p[2], new_s[0], new_s[1], new_s[2])
```

```python
import functools
import math

import numpy as np
import jax
import jax.numpy as jnp
from jax import lax
from jax.experimental import pallas as pl
from jax.experimental.pallas import tpu as pltpu

F32 = jnp.float32
BF16 = jnp.bfloat16

EPS = 1e-6
NEG = -1e30
CHUNK = 128
SGU_GROUPS = 8
HEAD_DIM = 64
N_HEADS_B = 8
WINDOWS = (128, 512, 2048)
DILATIONS = (1, 4, 16)
SPAN = 128
PEER_HEADS = 8
PEER_NKEYS = 128
PEER_TOPK = 16

LANES = 128
VMEM_LIMIT = 56 * 1024 * 1024


def _cparams(*sem):
    return pltpu.CompilerParams(dimension_semantics=sem, vmem_limit_bytes=VMEM_LIMIT)


def _rms(x, g):
    return x * lax.rsqrt(jnp.mean(x * x, axis=-1, keepdims=True) + EPS) * g


def _gelu(x):
    return 0.5 * x * (1.0 + lax.erf(x * (1.0 / math.sqrt(2.0))))


def _dot(a, b):
    return jnp.dot(a, b, preferred_element_type=F32)


def _dot_nt(a, b):
    return lax.dot_general(a, b, (((1,), (1,)), ((), ())), preferred_element_type=F32)


def _full(shape):
    nd = len(shape)
    return pl.BlockSpec(shape, lambda *_: (0,) * nd)


def _sgu_front(x, gmix_ref, win_ref, gv_ref):
    d = x.shape[-1]
    xn = _rms(x, gmix_ref[...]).astype(BF16)
    uv = _gelu(_dot(xn, win_ref[...]))
    return uv[:, :d], _rms(uv[:, d:], gv_ref[...])


def _sgu_seq_kernel(x_ref, gmix_ref, win_ref, gv_ref, ws_ref, bs_ref, wout_ref, h_ref, mixed_ref):
    x = x_ref[...]
    u, v = _sgu_front(x, gmix_ref, win_ref, gv_ref)
    vb = v.astype(BF16)
    gw = v.shape[-1] // SGU_GROUPS
    row = lax.broadcasted_iota(jnp.int32, (CHUNK, CHUNK), 0)
    col = lax.broadcasted_iota(jnp.int32, (CHUNK, CHUNK), 1)
    for g in range(SGU_GROUPS):
        wsg = jnp.where(col <= row, ws_ref[g], 0.0).astype(BF16)
        for c in range(x.shape[0] // CHUNK):
            rs = slice(c * CHUNK, (c + 1) * CHUNK)
            cs = slice(g * gw, (g + 1) * gw)
            mixed_ref[rs, cs] = _dot(wsg, vb[rs, cs]) + bs_ref[:, cs]
    z = (u * mixed_ref[...]).astype(BF16)
    h_ref[...] = x + _dot(z, wout_ref[...])


def _sgu_first_kernel(x_ref, gmix_ref, win_ref, gv_ref, ws0_ref, bs0_ref, wout_ref, h_ref, v_ref):
    x = x_ref[...]
    u, v = _sgu_front(x, gmix_ref, win_ref, gv_ref)
    v_ref[...] = v
    z = (u * (v * ws0_ref[...] + bs0_ref[...])).astype(BF16)
    h_ref[...] = x + _dot(z, wout_ref[...])


def _sgu_seq(x, gmix, win, gv, ws, bs_full, wout, tb):
    n, d = x.shape
    return pl.pallas_call(
        _sgu_seq_kernel,
        grid=(n // tb,),
        in_specs=[pl.BlockSpec((tb, d), lambda i: (i, 0)), _full(gmix.shape), _full(win.shape),
                  _full(gv.shape), _full(ws.shape), _full(bs_full.shape), _full(wout.shape)],
        out_specs=pl.BlockSpec((tb, d), lambda i: (i, 0)),
        out_shape=jax.ShapeDtypeStruct((n, d), F32),
        scratch_shapes=[pltpu.VMEM((tb, d), F32)],
        compiler_params=_cparams("parallel"),
        name="sgu_seq",
    )(x, gmix, win, gv, ws, bs_full, wout)


def _sgu_first(x, gmix, win, gv, ws0, bs0, wout):
    n, d = x.shape
    return pl.pallas_call(
        _sgu_first_kernel,
        grid=(1,),
        in_specs=[_full(x.shape), _full(gmix.shape), _full(win.shape), _full(gv.shape),
                  _full(ws0.shape), _full(bs0.shape), _full(wout.shape)],
        out_specs=[_full((n, d)), _full((n, d))],
        out_shape=[jax.ShapeDtypeStruct((n, d), F32), jax.ShapeDtypeStruct((n, d), F32)],
        compiler_params=_cparams("arbitrary"),
        name="sgu_first",
    )(x, gmix, win, gv, ws0, bs0, wout)


_CAND_COUNTS = [PEER_TOPK // (i + 1) for i in range(PEER_TOPK)]
TOKEN_GROUP = 16
TILE_ROW_STRIDE = 24


def _insert_top(entries, payloads):
    shape = entries[0].shape
    vals = [jnp.full(shape, -jnp.inf, F32)] * PEER_TOPK
    pays = [[jnp.zeros(shape, F32)] * PEER_TOPK for _ in payloads]
    for e, x in enumerate(entries):
        above = [x > v for v in vals]

        def shifted(old, new, j):
            if j == 0:
                return jnp.where(above[0], new, old[0])
            return jnp.where(above[j], jnp.where(above[j - 1], old[j - 1], new), old[j])

        pays = [[shifted(old, p[e], j) for j in range(PEER_TOPK)] for old, p in zip(pays, payloads)]
        vals = [shifted(vals, x, j) for j in range(PEER_TOPK)]
    return vals, pays


def _peer_select_kernel(h_ref, g_ref, wq_ref, sk_ref, xn_ref, i1_ref, i2_ref, gate_ref,
                        q_scr, sc1_scr, sc2_scr, a_scr, b_scr, g_scr):
    tb = h_ref.shape[0]
    chunks = tb // LANES
    xn = _rms(h_ref[...], g_ref[...]).astype(BF16)
    xn_ref[...] = xn
    q_scr[...] = _dot(xn, wq_ref[...])
    dk = sk_ref.shape[-1]

    def head_body(h, carry):
        halves = []
        for sc_scr, hc in ((sc1_scr, 2 * h), (sc2_scr, 2 * h + 1)):
            cols = pl.ds(pl.multiple_of(hc * dk, dk), dk)
            for s in range(chunks):
                qc = q_scr[s * LANES:(s + 1) * LANES, cols].astype(BF16)
                sc_scr[pl.ds(s, PEER_NKEYS, stride=chunks), :] = _dot_nt(sk_ref[hc], qc)
            keys = [sc_scr[k * chunks:(k + 1) * chunks, :] for k in range(PEER_NKEYS)]
            vals, (idx,) = _insert_top(keys, [[float(k) for k in range(PEER_NKEYS)]])
            halves.append((vals, idx))
        (t1, k1), (t2, k2) = halves
        pairs = [(i, j) for i, nj in enumerate(_CAND_COUNTS) for j in range(nj)]
        top, (ka, kb) = _insert_top([t1[i] + t2[j] for i, j in pairs],
                                    [[k1[i] for i, _ in pairs], [k2[j] for _, j in pairs]])
        e = [jnp.exp(t - top[0]) for t in top]
        total = functools.reduce(lambda x, y: x + y, e)
        stack = lambda tiles: jnp.concatenate(tiles, axis=0)
        a_scr[h] = stack(ka)
        b_scr[h] = stack(kb)
        g_scr[h] = stack([x / total for x in e])
        return carry

    lax.fori_loop(0, PEER_HEADS, head_body, 0)
    for out_ref, scr in ((i1_ref, a_scr), (i2_ref, b_scr), (gate_ref, g_scr)):
        for s in range(chunks):
            sel = jnp.concatenate([scr[h, pl.ds(s, PEER_TOPK, stride=chunks), :] for h in range(PEER_HEADS)], axis=0)
            out_ref[s * LANES:(s + 1) * LANES, :] = sel.T


def _peer_select(h, g, wq, sk, tb):
    n, d = h.shape
    nhc, nk, dk = sk.shape
    n_sel = PEER_HEADS * PEER_TOPK
    chunks = tb // LANES
    sel = pl.BlockSpec((tb, n_sel), lambda i: (i, 0))
    return pl.pallas_call(
        _peer_select_kernel,
        grid=(n // tb,),
        in_specs=[pl.BlockSpec((tb, d), lambda i: (i, 0)), _full(g.shape), _full(wq.shape), _full(sk.shape)],
        out_specs=[pl.BlockSpec((tb, d), lambda i: (i, 0)), sel, sel, sel],
        out_shape=[jax.ShapeDtypeStruct((n, d), BF16)] + [jax.ShapeDtypeStruct((n, n_sel), F32)] * 3,
        scratch_shapes=[pltpu.VMEM((tb, wq.shape[1]), F32),
                        pltpu.VMEM((nk * chunks, LANES), F32),
                        pltpu.VMEM((nk * chunks, LANES), F32)]
                       + [pltpu.VMEM((PEER_HEADS, PEER_TOPK * chunks, LANES), F32)] * 3,
        compiler_params=_cparams("parallel"),
        name="peer_select",
    )(h, g, wq, sk)


def _build_gates(i1_ref, i2_ref, g_ref, gates_ref, tile_ref):
    tb = i1_ref.shape[0]
    nk = PEER_NKEYS
    sub = lax.broadcasted_iota(jnp.int32, (nk, i1_ref.shape[1]), 0).astype(F32)

    def group_body(gi, carry):
        t0 = pl.multiple_of(gi * TOKEN_GROUP, TOKEN_GROUP)
        for tt in range(TOKEN_GROUP):
            row = pl.ds(t0 + tt, 1)
            first = jnp.where(sub == i1_ref[row, :], g_ref[row, :], 0.0).astype(BF16)
            second = jnp.where(sub == i2_ref[row, :], 1.0, 0.0).astype(BF16)
            tile_ref[pl.ds(tt, nk, stride=TILE_ROW_STRIDE), :] = _dot_nt(first, second)
        for a in range(nk):
            rows = tile_ref[a * TILE_ROW_STRIDE:a * TILE_ROW_STRIDE + TOKEN_GROUP, :]
            gates_ref[pl.ds(t0, TOKEN_GROUP), a * nk:(a + 1) * nk] = rows.astype(BF16)
        return carry

    lax.fori_loop(0, tb // TOKEN_GROUP, group_body, 0)


def _peer_dense_kernel(xn_ref, i1_ref, i2_ref, g_ref, u_ref, v_ref, h_ref, o_ref, acc_ref, gates_ref, tile_ref):
    j = pl.program_id(1)
    eb = u_ref.shape[0]

    @pl.when(j == 0)
    def _():
        acc_ref[...] = jnp.zeros_like(acc_ref)
        _build_gates(i1_ref, i2_ref, g_ref, gates_ref, tile_ref)

    act = _gelu(_dot_nt(xn_ref[...], u_ref[...]))
    gates = gates_ref[:, pl.ds(pl.multiple_of(j * eb, eb), eb)]
    acc_ref[...] += _dot((gates.astype(F32) * act).astype(BF16), v_ref[...])

    @pl.when(j == pl.num_programs(1) - 1)
    def _():
        o_ref[...] = h_ref[...] + acc_ref[...]


def _peer_dense(xn, i1, i2, gate, u, v, h, tb, eb):
    n, d = h.shape
    n_exp = u.shape[0]
    tok = lambda w: pl.BlockSpec((tb, w), lambda i, j: (i, 0))
    return pl.pallas_call(
        _peer_dense_kernel,
        grid=(n // tb, n_exp // eb),
        in_specs=[tok(d), tok(i1.shape[1]), tok(i1.shape[1]), tok(i1.shape[1]),
                  pl.BlockSpec((eb, d), lambda i, j: (j, 0)),
                  pl.BlockSpec((eb, d), lambda i, j: (j, 0)),
                  tok(d)],
        out_specs=tok(d),
        out_shape=jax.ShapeDtypeStruct((n, d), F32),
        scratch_shapes=[pltpu.VMEM((tb, d), F32), pltpu.VMEM((tb, n_exp), BF16),
                        pltpu.VMEM((PEER_NKEYS * TILE_ROW_STRIDE, PEER_NKEYS), F32)],
        compiler_params=_cparams("parallel", "arbitrary"),
        name="peer_dense",
    )(xn, i1, i2, gate, u, v, h)


def _peer(h, g, wq, sk, u, v, tb_sel, tb, eb):
    xn, i1, i2, gate = _peer_select(h, g, wq, sk, tb_sel)
    return _peer_dense(xn, i1, i2, gate, u, v, h, tb, eb)


def _ple_kernel(h_ref, p_ref, g_ref, wg_ref, wp_ref, gf_ref, o_ref, *, final):
    h = h_ref[...]
    gate = jax.nn.sigmoid(_dot(_rms(h, g_ref[...]).astype(BF16), wg_ref[...]))
    h = h + gate * _dot(p_ref[...].astype(BF16), wp_ref[...])
    o_ref[...] = _rms(h, gf_ref[...]) if final else h


def _ple(h, p, g, wg, wp, gf, tb, final):
    n, d = h.shape
    return pl.pallas_call(
        functools.partial(_ple_kernel, final=final),
        grid=(n // tb,),
        in_specs=[pl.BlockSpec((tb, d), lambda i: (i, 0)), pl.BlockSpec((tb, p.shape[1]), lambda i: (i, 0)),
                  _full(g.shape), _full(wg.shape), _full(wp.shape), _full(gf.shape)],
        out_specs=pl.BlockSpec((tb, d), lambda i: (i, 0)),
        out_shape=jax.ShapeDtypeStruct((n, d), F32),
        compiler_params=_cparams("parallel"),
        name="ple_final" if final else "ple",
    )(h, p, g, wg, wp, gf)


def _qkv_kernel(h_ref, gm_ref, gkv_ref, wq_ref, wkv_ref, q_ref, kv_ref):
    h = h_ref[...]
    q_ref[...] = _dot(_rms(h, gm_ref[...]).astype(BF16), wq_ref[...])
    kv_ref[...] = _dot(_rms(h, gkv_ref[...]).astype(BF16), wkv_ref[...])


def _qkv(h, gm, gkv, wq, wkv, tb):
    n, d = h.shape
    return pl.pallas_call(
        _qkv_kernel,
        grid=(n // tb,),
        in_specs=[pl.BlockSpec((tb, d), lambda i: (i, 0)), _full(gm.shape), _full(gkv.shape),
                  _full(wq.shape), _full(wkv.shape)],
        out_specs=[pl.BlockSpec((tb, wq.shape[1]), lambda i: (i, 0)),
                   pl.BlockSpec((tb, wkv.shape[1]), lambda i: (i, 0))],
        out_shape=[jax.ShapeDtypeStruct((n, wq.shape[1]), F32),
                   jax.ShapeDtypeStruct((n, wkv.shape[1]), F32)],
        compiler_params=_cparams("parallel"),
        name="qkv",
    )(h, gm, gkv, wq, wkv)


def _qkv_strided_kernel(h_ref, gm_ref, gkv_ref, wq_ref, wkv_ref, *refs):
    n_br = len(DILATIONS)
    tail_ref, outs, (q_scr, kv_scr) = refs[0], refs[1:1 + 3 * n_br], refs[1 + 3 * n_br:]
    tb = h_ref.shape[0]
    hw = N_HEADS_B * HEAD_DIM
    tiles = hw // LANES
    h = h_ref[...]
    q = _dot(_rms(h, gm_ref[...]).astype(BF16), wq_ref[...])
    kv = _dot(_rms(h, gkv_ref[...]).astype(BF16), wkv_ref[...])
    tail_ref[...] = kv
    for j in range(q_scr.shape[0]):
        q_scr[j] = q[:, j * LANES:(j + 1) * LANES]
    for j in range(kv_scr.shape[0]):
        kv_scr[j] = kv[:, j * LANES:(j + 1) * LANES]
    for g, r in enumerate(DILATIONS):
        q_ref, k_ref, v_ref = outs[3 * g:3 * g + 3]
        for c in range(r):
            rows = pl.ds(c, tb // r, stride=r)
            for j in range(tiles):
                cols = slice(j * LANES, (j + 1) * LANES)
                q_ref[c, :, cols] = q_scr[g * tiles + j, rows, :].astype(BF16)
                k_ref[c, :, cols] = kv_scr[2 * g * tiles + j, rows, :].astype(BF16)
                v_ref[c, :, cols] = kv_scr[(2 * g + 1) * tiles + j, rows, :].astype(BF16)


def _qkv_strided(h, gm, gkv, wq, wkv, tb, tail_rows):
    n, d = h.shape
    hw = N_HEADS_B * HEAD_DIM
    first_tail_block = (n - tail_rows) // tb
    out_specs = [pl.BlockSpec((tb, wkv.shape[1]), lambda i: (jnp.maximum(i - first_tail_block, 0), 0))]
    out_shape = [jax.ShapeDtypeStruct((tail_rows, wkv.shape[1]), F32)]
    for r in DILATIONS:
        out_specs += [pl.BlockSpec((r, tb // r, hw), lambda i: (0, i, 0))] * 3
        out_shape += [jax.ShapeDtypeStruct((r, n // r, hw), BF16)] * 3
    outs = pl.pallas_call(
        _qkv_strided_kernel,
        grid=(n // tb,),
        in_specs=[pl.BlockSpec((tb, d), lambda i: (i, 0)), _full(gm.shape), _full(gkv.shape),
                  _full(wq.shape), _full(wkv.shape)],
        out_specs=out_specs,
        out_shape=out_shape,
        scratch_shapes=[pltpu.VMEM((wq.shape[1] // LANES, tb, LANES), F32),
                        pltpu.VMEM((wkv.shape[1] // LANES, tb, LANES), F32)],
        compiler_params=_cparams("arbitrary"),
        name="qkv_strided",
    )(h, gm, gkv, wq, wkv)
    return outs[0], [outs[1 + 3 * g:4 + 3 * g] for g in range(len(DILATIONS))]


def _alibi_slopes():
    n = len(WINDOWS) * N_HEADS_B
    e = (np.arange(n, dtype=np.float32) + 1.0) * np.float32(8.0 / n)
    return np.exp2(-e).astype(np.float32).reshape(len(WINDOWS), N_HEADS_B)


def _attn_seq_kernel(q_ref, kc_ref, kp_ref, vc_ref, vp_ref, o_ref, lse_ref, *, dilation, slopes):
    b = pl.program_id(1)
    qi = lax.broadcasted_iota(jnp.int32, (SPAN, SPAN), 0)
    ki = lax.broadcasted_iota(jnp.int32, (SPAN, SPAN), 1)
    dist_c = (qi - ki).astype(F32) * float(dilation)
    dist_p = (qi - ki + SPAN).astype(F32) * float(dilation)
    valid_c = ki <= qi
    valid_p = jnp.logical_and(ki >= qi, b > 0)
    scale = HEAD_DIM ** -0.5
    q, kc, kp, vc, vp = q_ref[...], kc_ref[...], kp_ref[...], vc_ref[...], vp_ref[...]
    heads = [slice(h * HEAD_DIM, (h + 1) * HEAD_DIM) for h in range(N_HEADS_B)]
    raw = [(_dot_nt(q[:, hs], kc[:, hs]), _dot_nt(q[:, hs], kp[:, hs])) for hs in heads]
    probs, outs, lses = [], [], []
    for h, (s_c, s_p) in enumerate(raw):
        s_c = jnp.where(valid_c, s_c * scale - slopes[h] * dist_c, NEG)
        s_p = jnp.where(valid_p, s_p * scale - slopes[h] * dist_p, NEG)
        m = jnp.maximum(jnp.max(s_c, axis=-1, keepdims=True), jnp.max(s_p, axis=-1, keepdims=True))
        e_c = jnp.exp(s_c - m)
        e_p = jnp.exp(s_p - m)
        l = jnp.sum(e_c, axis=-1, keepdims=True) + jnp.sum(e_p, axis=-1, keepdims=True)
        probs.append((e_c.astype(BF16), e_p.astype(BF16), l))
        lses.append(jnp.broadcast_to(m + jnp.log(l), (SPAN, HEAD_DIM)))
    for hs, (e_c, e_p, l) in zip(heads, probs):
        outs.append((_dot(e_c, vc[:, hs]) + _dot(e_p, vp[:, hs])) / l)
    o_ref[...] = jnp.concatenate(outs, axis=1)
    lse_ref[...] = jnp.concatenate(lses, axis=1)


def _attn_seq(q, k, v, branch):
    r, length, hw = q.shape
    cur = pl.BlockSpec((None, SPAN, hw), lambda c, b: (c, b, 0))
    prev = pl.BlockSpec((None, SPAN, hw), lambda c, b: (c, jnp.maximum(b - 1, 0), 0))
    return pl.pallas_call(
        functools.partial(_attn_seq_kernel, dilation=r, slopes=[float(s) for s in _alibi_slopes()[branch]]),
        grid=(r, length // SPAN),
        in_specs=[cur, cur, prev, cur, prev],
        out_specs=[cur, cur],
        out_shape=[jax.ShapeDtypeStruct((r, length, hw), F32)] * 2,
        compiler_params=_cparams("parallel", "arbitrary"),
        name=f"attn_seq_w{WINDOWS[branch]}",
    )(q, k, k, v, v)


STEP_HEAD_GROUPS = 2


def _attn_step_kernel(q_ref, kvn_ref, sl_ref, *refs):
    n_br = len(WINDOWS)
    st_refs, new_refs, (o_ref, lse_ref) = refs[:n_br], refs[n_br:2 * n_br], refs[2 * n_br:]
    rows = q_ref.shape[-2]
    heads = rows // HEAD_DIM
    scale = HEAD_DIM ** -0.5
    per_head = lambda a: a.reshape(heads, HEAD_DIM, a.shape[-1])
    for g in range(n_br):
        w_len = st_refs[g].shape[-1]
        q = per_head(q_ref[0, g, 0])
        k_new, v_new = kvn_ref[0, g, 0, 0], kvn_ref[0, g, 1, 0]
        k_old, v_old = st_refs[g][0, 0, 0], st_refs[g][0, 1, 0]
        slope = per_head(sl_ref[g, 0])[:, 0:1, :]
        pos = lax.broadcasted_iota(jnp.int32, (heads, 1, w_len), 2)
        s_old = jnp.sum(per_head(k_old) * q, axis=1, keepdims=True) * scale - slope * (w_len - pos).astype(F32)
        s_old = jnp.where(lax.rem(pos, DILATIONS[g]) == 0, s_old, NEG)
        s_new = jnp.sum(per_head(k_new) * q, axis=1, keepdims=True) * scale
        m = jnp.maximum(jnp.max(s_old, axis=2, keepdims=True), s_new)
        e_old = jnp.exp(s_old - m)
        e_new = jnp.exp(s_new - m)
        l = jnp.sum(e_old, axis=2, keepdims=True) + e_new
        o = (jnp.sum(e_old * per_head(v_old), axis=2, keepdims=True) + e_new * per_head(v_new)) / l
        o_ref[0, g, 0] = o.reshape(rows, 1)
        lse_ref[0, g, 0] = jnp.broadcast_to(m + jnp.log(l), o.shape).reshape(rows, 1)
        last = lax.broadcasted_iota(jnp.int32, (rows, w_len), 1) == w_len - 1
        for kv, old, new in ((0, k_old, k_new), (1, v_old, v_new)):
            new_refs[g][0, kv, 0] = jnp.where(last, new, pltpu.roll(old, w_len - 1, axis=1))


def _attn_step(q, kv_new, states):
    nb, n_br = q.shape[:2]
    hg = STEP_HEAD_GROUPS
    rows = N_HEADS_B // hg * HEAD_DIM
    slopes = jnp.asarray(np.repeat(_alibi_slopes(), HEAD_DIM, axis=1).reshape(n_br, hg, rows, 1))
    st_args, st_specs = [], []
    for g in range(n_br):
        assert states[g].shape[1] == WINDOWS[g], "window buffers must be full"
        st_args.append(jnp.transpose(states[g], (0, 2, 3, 4, 1)).reshape(nb, 2, hg, rows, WINDOWS[g]))
        st_specs.append(pl.BlockSpec((1, 2, 1, rows, WINDOWS[g]), lambda n, j: (n, 0, j, 0, 0)))
    q = q.reshape(nb, n_br, hg, rows, 1)
    kv_new = kv_new.reshape(nb, n_br, 2, hg, rows, 1)
    vec = pl.BlockSpec((1, n_br, 1, rows, 1), lambda n, j: (n, 0, j, 0, 0))
    outs = pl.pallas_call(
        _attn_step_kernel,
        grid=(nb, hg),
        in_specs=[vec, pl.BlockSpec((1, n_br, 2, 1, rows, 1), lambda n, j: (n, 0, 0, j, 0, 0)),
                  pl.BlockSpec((n_br, 1, rows, 1), lambda n, j: (0, j, 0, 0))] + st_specs,
        out_specs=st_specs + [vec, vec],
        out_shape=[jax.ShapeDtypeStruct(a.shape, F32) for a in st_args] + [jax.ShapeDtypeStruct(q.shape, F32)] * 2,
        compiler_params=_cparams("parallel", "parallel"),
        name="attn_step",
    )(q, kv_new, slopes, *st_args)
    new_states = [jnp.transpose(a.reshape(nb, 2, N_HEADS_B, HEAD_DIM, WINDOWS[g]), (0, 4, 1, 2, 3))
                  for g, a in enumerate(outs[:n_br])]
    o, lse = (a.reshape(nb, n_br, N_HEADS_B * HEAD_DIM) for a in outs[n_br:])
    return o, lse, new_states


def _attn_out_kernel(*refs, dilations):
    n_br = len(dilations)
    o_refs, lse_refs = refs[:n_br], refs[n_br:2 * n_br]
    h_ref, wo_ref, out_ref = refs[2 * n_br:2 * n_br + 3]
    scratch = refs[2 * n_br + 3:]
    tb = h_ref.shape[0]

    def token_order(ref, scr_ref, r):
        if r == 1:
            return ref[0]
        tiles = scr_ref.shape[0]
        for c in range(r):
            for j in range(tiles):
                scr_ref[j, pl.ds(c, tb // r, stride=r), :] = ref[c, :, j * LANES:(j + 1) * LANES]
        return jnp.concatenate([scr_ref[j] for j in range(tiles)], axis=1)

    lses = [token_order(ref, scratch[g], dilations[g]) for g, ref in enumerate(lse_refs)]
    m = functools.reduce(jnp.maximum, lses)
    ws = [jnp.exp(l - m) for l in lses]
    tot = functools.reduce(lambda a, b: a + b, ws)
    o = functools.reduce(lambda a, b: a + b, [w / tot * token_order(ref, scratch[n_br + g], dilations[g])
                                              for g, (w, ref) in enumerate(zip(ws, o_refs))])
    out_ref[...] = h_ref[...] + _dot(o.astype(BF16), wo_ref[...])


def _attn_out(os, lses, h, wo, tb, dilations):
    n, d = h.shape
    hw = wo.shape[0]
    specs = [pl.BlockSpec((r, tb // r, hw), lambda i: (0, i, 0)) for r in dilations]
    return pl.pallas_call(
        functools.partial(_attn_out_kernel, dilations=dilations),
        grid=(n // tb,),
        in_specs=specs + specs + [pl.BlockSpec((tb, d), lambda i: (i, 0)), _full(wo.shape)],
        out_specs=pl.BlockSpec((tb, d), lambda i: (i, 0)),
        out_shape=jax.ShapeDtypeStruct((n, d), F32),
        scratch_shapes=[pltpu.VMEM((hw // LANES, tb, LANES), F32)] * (2 * len(dilations)),
        compiler_params=_cparams("parallel"),
        name="attn_out",
    )(*os, *lses, h, wo)


def kernel(x_prompt, x_sample, state_kv_w128, state_kv_w512, state_kv_w2048, p_prompt, p_sample, g_mix, sgu_w_in, sgu_g_v, sgu_w_s, sgu_b_s, sgu_w_out, kv_g, w_kv, attn_w_q, attn_w_o, g_ffn, peer_w_query, peer_subkeys, peer_u, peer_v, ple_g, ple_w_gate, ple_w_proj, g_final):
    depth, d = g_mix.shape
    assert depth == 2 and sgu_w_in.shape[0] == 1 and attn_w_q.shape[0] == 1
    b_p, s_p, _ = x_prompt.shape
    b_s, t_s, _ = x_sample.shape
    assert b_p == 1 and t_s == 1 and s_p % max(WINDOWS) == 0
    states = (state_kv_w128, state_kv_w512, state_kv_w2048)
    hw = N_HEADS_B * HEAD_DIM
    n_br = len(WINDOWS)

    row = lambda a: a.reshape(1, -1).astype(F32)
    bf = lambda a: a.astype(BF16)

    w_in, w_out = bf(sgu_w_in[0]), bf(sgu_w_out[0])
    gw = d // SGU_GROUPS
    bs_full = jnp.repeat(sgu_b_s[0].T, gw, axis=1)
    ws0 = jnp.repeat(sgu_w_s[0][:, 0, 0], gw).reshape(1, d)
    bs0 = bs_full[0:1]
    wq_peer = bf(peer_w_query)
    subkeys = bf(peer_subkeys.reshape(depth, 2 * PEER_HEADS, PEER_NKEYS, -1))
    u_tab = bf(peer_u)
    v_tab = bf(peer_v)
    w_gate, w_proj = bf(ple_w_gate), bf(ple_w_proj)
    w_q, w_kvb, w_o = bf(attn_w_q[0]), bf(w_kv), bf(attn_w_o[0])

    n_s = LANES
    xs = jnp.pad(x_sample.reshape(b_s, d), ((0, n_s - b_s), (0, 0)))
    ps = jnp.pad(p_sample.reshape(depth, b_s, -1), ((0, 0), (0, n_s - b_s), (0, 0)))
    xp = x_prompt.reshape(s_p, d)
    pp = p_prompt.reshape(depth, s_p, -1)

    def channel(h, p, i, tb_sel, tb, tbp, final):
        h = _peer(h, row(g_ffn[i]), wq_peer[i], subkeys[i], u_tab[i], v_tab[i], tb_sel, tb, 2048)
        return _ple(h, p[i], row(ple_g[i]), w_gate[i], w_proj[i], row(g_final), tbp, final)

    hp = _sgu_seq(xp, row(g_mix[0]), w_in, row(sgu_g_v[0]), sgu_w_s[0], bs_full, w_out, 512)
    hs, v_s = _sgu_first(xs, row(g_mix[0]), w_in, row(sgu_g_v[0]), ws0, bs0, w_out)
    hp = channel(hp, pp, 0, 1024, 512, 512, False)
    hs = channel(hs, ps, 0, n_s, n_s, n_s, False)

    tail = min(max(WINDOWS), s_p)
    kv_tail, qkv_p = _qkv_strided(hp, row(g_mix[1]), row(kv_g), w_q, w_kvb, 512, tail)
    qs, kvs = _qkv(hs, row(g_mix[1]), row(kv_g), w_q, w_kvb, n_s)
    outs = [_attn_seq(*qkv_p[g], g) for g in range(n_br)]
    hp = _attn_out([o for o, _ in outs], [l for _, l in outs], hp, w_o, 512, DILATIONS)
    q_step = qs[:b_s].reshape(b_s, n_br, N_HEADS_B, HEAD_DIM)
    kv_step = kvs[:b_s].reshape(b_s, n_br, 2, N_HEADS_B, HEAD_DIM)
    o_s, lse_s, new_s = _attn_step(q_step, kv_step, states)
    pad = lambda a, g: jnp.pad(a[:, g].reshape(1, b_s, hw), ((0, 0), (0, n_s - b_s), (0, 0)))
    hs = _attn_out([pad(o_s, g) for g in range(n_br)], [pad(lse_s, g) for g in range(n_br)],
                   hs, w_o, n_s, (1,) * n_br)
    yp = channel(hp, pp, 1, 1024, 512, 512, True)
    ys = channel(hs, ps, 1, n_s, n_s, n_s, True)

    kv_p = kv_tail.reshape(1, tail, n_br, 2, N_HEADS_B, HEAD_DIM)
    new_p = [kv_p[:, tail - min(WINDOWS[g], s_p):, g] for g in range(n_br)]
    return (yp.reshape(1, s_p, d), ys[:b_s].reshape(b_s, 1, d), v_s[:b_s].reshape(1, b_s, 1, d),
            new_p[0], new_p[1], new_p[2], new_s[0], new_s[1], new_s[2])
```

```python
import functools
import math

import numpy as np
import jax
import jax.numpy as jnp
from jax import lax
from jax.experimental import pallas as pl
from jax.experimental.pallas import tpu as pltpu

F32 = jnp.float32
BF16 = jnp.bfloat16

EPS = 1e-6
NEG = -1e30
CHUNK = 128
SGU_GROUPS = 8
HEAD_DIM = 64
N_HEADS_B = 8
WINDOWS = (128, 512, 2048)
DILATIONS = (1, 4, 16)
SPAN = 128
PEER_HEADS = 8
PEER_NKEYS = 128
PEER_TOPK = 16

LANES = 128
VMEM_LIMIT = 56 * 1024 * 1024


def _cparams(*sem):
    return pltpu.CompilerParams(dimension_semantics=sem, vmem_limit_bytes=VMEM_LIMIT)


def _rms(x, g):
    return x * lax.rsqrt(jnp.mean(x * x, axis=-1, keepdims=True) + EPS) * g


def _gelu(x):
    return 0.5 * x * (1.0 + lax.erf(x * (1.0 / math.sqrt(2.0))))


def _dot(a, b):
    return jnp.dot(a, b, preferred_element_type=F32)


def _dot_nt(a, b):
    return lax.dot_general(a, b, (((1,), (1,)), ((), ())), preferred_element_type=F32)


def _full(shape):
    nd = len(shape)
    return pl.BlockSpec(shape, lambda *_: (0,) * nd)


def _sgu_front(x, gmix_ref, win_ref, gv_ref):
    d = x.shape[-1]
    xn = _rms(x, gmix_ref[...]).astype(BF16)
    uv = _gelu(_dot(xn, win_ref[...]))
    return uv[:, :d], _rms(uv[:, d:], gv_ref[...])


def _sgu_seq_kernel(x_ref, gmix_ref, win_ref, gv_ref, ws_ref, bs_ref, wout_ref, h_ref, mixed_ref):
    x = x_ref[...]
    u, v = _sgu_front(x, gmix_ref, win_ref, gv_ref)
    vb = v.astype(BF16)
    gw = v.shape[-1] // SGU_GROUPS
    row = lax.broadcasted_iota(jnp.int32, (CHUNK, CHUNK), 0)
    col = lax.broadcasted_iota(jnp.int32, (CHUNK, CHUNK), 1)
    for g in range(SGU_GROUPS):
        wsg = jnp.where(col <= row, ws_ref[g], 0.0).astype(BF16)
        for c in range(x.shape[0] // CHUNK):
            rs = slice(c * CHUNK, (c + 1) * CHUNK)
            cs = slice(g * gw, (g + 1) * gw)
            mixed_ref[rs, cs] = _dot(wsg, vb[rs, cs]) + bs_ref[:, cs]
    z = (u * mixed_ref[...]).astype(BF16)
    h_ref[...] = x + _dot(z, wout_ref[...])


def _sgu_first_kernel(x_ref, gmix_ref, win_ref, gv_ref, ws0_ref, bs0_ref, wout_ref, h_ref, v_ref):
    x = x_ref[...]
    u, v = _sgu_front(x, gmix_ref, win_ref, gv_ref)
    v_ref[...] = v
    z = (u * (v * ws0_ref[...] + bs0_ref[...])).astype(BF16)
    h_ref[...] = x + _dot(z, wout_ref[...])


def _sgu_seq(x, gmix, win, gv, ws, bs_full, wout, tb):
    n, d = x.shape
    return pl.pallas_call(
        _sgu_seq_kernel,
        grid=(n // tb,),
        in_specs=[pl.BlockSpec((tb, d), lambda i: (i, 0)), _full(gmix.shape), _full(win.shape),
                  _full(gv.shape), _full(ws.shape), _full(bs_full.shape), _full(wout.shape)],
        out_specs=pl.BlockSpec((tb, d), lambda i: (i, 0)),
        out_shape=jax.ShapeDtypeStruct((n, d), F32),
        scratch_shapes=[pltpu.VMEM((tb, d), F32)],
        compiler_params=_cparams("parallel"),
        name="sgu_seq",
    )(x, gmix, win, gv, ws, bs_full, wout)


def _sgu_first(x, gmix, win, gv, ws0, bs0, wout):
    n, d = x.shape
    return pl.pallas_call(
        _sgu_first_kernel,
        grid=(1,),
        in_specs=[_full(x.shape), _full(gmix.shape), _full(win.shape), _full(gv.shape),
                  _full(ws0.shape), _full(bs0.shape), _full(wout.shape)],
        out_specs=[_full((n, d)), _full((n, d))],
        out_shape=[jax.ShapeDtypeStruct((n, d), F32), jax.ShapeDtypeStruct((n, d), F32)],
        compiler_params=_cparams("arbitrary"),
        name="sgu_first",
    )(x, gmix, win, gv, ws0, bs0, wout)


_CAND_COUNTS = [PEER_TOPK // (i + 1) for i in range(PEER_TOPK)]
TOKEN_GROUP = 16
TILE_ROW_STRIDE = 24


def _insert_top(entries, payloads):
    shape = entries[0].shape
    vals = [jnp.full(shape, -jnp.inf, F32)] * PEER_TOPK
    pays = [[jnp.zeros(shape, F32)] * PEER_TOPK for _ in payloads]
    for e, x in enumerate(entries):
        above = [x > v for v in vals]

        def shifted(old, new, j):
            if j == 0:
                return jnp.where(above[0], new, old[0])
            return jnp.where(above[j], jnp.where(above[j - 1], old[j - 1], new), old[j])

        pays = [[shifted(old, p[e], j) for j in range(PEER_TOPK)] for old, p in zip(pays, payloads)]
        vals = [shifted(vals, x, j) for j in range(PEER_TOPK)]
    return vals, pays


def _peer_select_kernel(h_ref, g_ref, wq_ref, sk_ref, xn_ref, i1_ref, i2_ref, gate_ref,
                        q_scr, sc1_scr, sc2_scr, a_scr, b_scr, g_scr):
    tb = h_ref.shape[0]
    chunks = tb // LANES
    xn = _rms(h_ref[...], g_ref[...]).astype(BF16)
    xn_ref[...] = xn
    q_scr[...] = _dot(xn, wq_ref[...])
    dk = sk_ref.shape[-1]

    def head_body(h, carry):
        halves = []
        for sc_scr, hc in ((sc1_scr, 2 * h), (sc2_scr, 2 * h + 1)):
            cols = pl.ds(pl.multiple_of(hc * dk, dk), dk)
            for s in range(chunks):
                qc = q_scr[s * LANES:(s + 1) * LANES, cols].astype(BF16)
                sc_scr[pl.ds(s, PEER_NKEYS, stride=chunks), :] = _dot_nt(sk_ref[hc], qc)
            keys = [sc_scr[k * chunks:(k + 1) * chunks, :] for k in range(PEER_NKEYS)]
            vals, (idx,) = _insert_top(keys, [[float(k) for k in range(PEER_NKEYS)]])
            halves.append((vals, idx))
        (t1, k1), (t2, k2) = halves
        pairs = [(i, j) for i, nj in enumerate(_CAND_COUNTS) for j in range(nj)]
        top, (ka, kb) = _insert_top([t1[i] + t2[j] for i, j in pairs],
                                    [[k1[i] for i, _ in pairs], [k2[j] for _, j in pairs]])
        e = [jnp.exp(t - top[0]) for t in top]
        total = functools.reduce(lambda x, y: x + y, e)
        stack = lambda tiles: jnp.concatenate(tiles, axis=0)
        a_scr[h] = stack(ka)
        b_scr[h] = stack(kb)
        g_scr[h] = stack([x / total for x in e])
        return carry

    lax.fori_loop(0, PEER_HEADS, head_body, 0)
    for out_ref, scr in ((i1_ref, a_scr), (i2_ref, b_scr), (gate_ref, g_scr)):
        for s in range(chunks):
            sel = jnp.concatenate([scr[h, pl.ds(s, PEER_TOPK, stride=chunks), :] for h in range(PEER_HEADS)], axis=0)
            out_ref[s * LANES:(s + 1) * LANES, :] = sel.T


def _peer_select(h, g, wq, sk, tb):
    n, d = h.shape
    nhc, nk, dk = sk.shape
    n_sel = PEER_HEADS * PEER_TOPK
    chunks = tb // LANES
    sel = pl.BlockSpec((tb, n_sel), lambda i: (i, 0))
    return pl.pallas_call(
        _peer_select_kernel,
        grid=(n // tb,),
        in_specs=[pl.BlockSpec((tb, d), lambda i: (i, 0)), _full(g.shape), _full(wq.shape), _full(sk.shape)],
        out_specs=[pl.BlockSpec((tb, d), lambda i: (i, 0)), sel, sel, sel],
        out_shape=[jax.ShapeDtypeStruct((n, d), BF16)] + [jax.ShapeDtypeStruct((n, n_sel), F32)] * 3,
        scratch_shapes=[pltpu.VMEM((tb, wq.shape[1]), F32),
                        pltpu.VMEM((nk * chunks, LANES), F32),
                        pltpu.VMEM((nk * chunks, LANES), F32)]
                       + [pltpu.VMEM((PEER_HEADS, PEER_TOPK * chunks, LANES), F32)] * 3,
        compiler_params=_cparams("parallel"),
        name="peer_select",
    )(h, g, wq, sk)


def _build_gates(i1_ref, i2_ref, g_ref, gates_ref, tile_ref):
    tb = i1_ref.shape[0]
    nk = PEER_NKEYS
    sub = lax.broadcasted_iota(jnp.int32, (nk, i1_ref.shape[1]), 0).astype(F32)

    def group_body(gi, carry):
        t0 = pl.multiple_of(gi * TOKEN_GROUP, TOKEN_GROUP)
        for tt in range(TOKEN_GROUP):
            row = pl.ds(t0 + tt, 1)
            first = jnp.where(sub == i1_ref[row, :], g_ref[row, :], 0.0).astype(BF16)
            second = jnp.where(sub == i2_ref[row, :], 1.0, 0.0).astype(BF16)
            tile_ref[pl.ds(tt, nk, stride=TILE_ROW_STRIDE), :] = _dot_nt(first, second)
        for a in range(nk):
            rows = tile_ref[a * TILE_ROW_STRIDE:a * TILE_ROW_STRIDE + TOKEN_GROUP, :]
            gates_ref[pl.ds(t0, TOKEN_GROUP), a * nk:(a + 1) * nk] = rows.astype(BF16)
        return carry

    lax.fori_loop(0, tb // TOKEN_GROUP, group_body, 0)


def _peer_dense_kernel(xn_ref, i1_ref, i2_ref, g_ref, u_ref, v_ref, h_ref, o_ref, acc_ref, gates_ref, tile_ref):
    j = pl.program_id(1)
    eb = u_ref.shape[0]

    @pl.when(j == 0)
    def _():
        acc_ref[...] = jnp.zeros_like(acc_ref)
        _build_gates(i1_ref, i2_ref, g_ref, gates_ref, tile_ref)

    act = _gelu(_dot_nt(xn_ref[...], u_ref[...]))
    gates = gates_ref[:, pl.ds(pl.multiple_of(j * eb, eb), eb)]
    acc_ref[...] += _dot((gates.astype(F32) * act).astype(BF16), v_ref[...])

    @pl.when(j == pl.num_programs(1) - 1)
    def _():
        o_ref[...] = h_ref[...] + acc_ref[...]


def _peer_dense(xn, i1, i2, gate, u, v, h, tb, eb):
    n, d = h.shape
    n_exp = u.shape[0]
    tok = lambda w: pl.BlockSpec((tb, w), lambda i, j: (i, 0))
    return pl.pallas_call(
        _peer_dense_kernel,
        grid=(n // tb, n_exp // eb),
        in_specs=[tok(d), tok(i1.shape[1]), tok(i1.shape[1]), tok(i1.shape[1]),
                  pl.BlockSpec((eb, d), lambda i, j: (j, 0)),
                  pl.BlockSpec((eb, d), lambda i, j: (j, 0)),
                  tok(d)],
        out_specs=tok(d),
        out_shape=jax.ShapeDtypeStruct((n, d), F32),
        scratch_shapes=[pltpu.VMEM((tb, d), F32), pltpu.VMEM((tb, n_exp), BF16),
                        pltpu.VMEM((PEER_NKEYS * TILE_ROW_STRIDE, PEER_NKEYS), F32)],
        compiler_params=_cparams("parallel", "arbitrary"),
        name="peer_dense",
    )(xn, i1, i2, gate, u, v, h)


def _peer(h, g, wq, sk, u, v, tb_sel, tb, eb):
    xn, i1, i2, gate = _peer_select(h, g, wq, sk, tb_sel)
    return _peer_dense(xn, i1, i2, gate, u, v, h, tb, eb)


def _ple_kernel(h_ref, p_ref, g_ref, wg_ref, wp_ref, gf_ref, o_ref, *, final):
    h = h_ref[...]
    gate = jax.nn.sigmoid(_dot(_rms(h, g_ref[...]).astype(BF16), wg_ref[...]))
    h = h + gate * _dot(p_ref[...].astype(BF16), wp_ref[...])
    o_ref[...] = _rms(h, gf_ref[...]) if final else h


def _ple(h, p, g, wg, wp, gf, tb, final):
    n, d = h.shape
    return pl.pallas_call(
        functools.partial(_ple_kernel, final=final),
        grid=(n // tb,),
        in_specs=[pl.BlockSpec((tb, d), lambda i: (i, 0)), pl.BlockSpec((tb, p.shape[1]), lambda i: (i, 0)),
                  _full(g.shape), _full(wg.shape), _full(wp.shape), _full(gf.shape)],
        out_specs=pl.BlockSpec((tb, d), lambda i: (i, 0)),
        out_shape=jax.ShapeDtypeStruct((n, d), F32),
        compiler_params=_cparams("parallel"),
        name="ple_final" if final else "ple",
    )(h, p, g, wg, wp, gf)


def _qkv_kernel(h_ref, gm_ref, gkv_ref, wq_ref, wkv_ref, q_ref, kv_ref):
    h = h_ref[...]
    q_ref[...] = _dot(_rms(h, gm_ref[...]).astype(BF16), wq_ref[...])
    kv_ref[...] = _dot(_rms(h, gkv_ref[...]).astype(BF16), wkv_ref[...])


def _qkv(h, gm, gkv, wq, wkv, tb):
    n, d = h.shape
    return pl.pallas_call(
        _qkv_kernel,
        grid=(n // tb,),
        in_specs=[pl.BlockSpec((tb, d), lambda i: (i, 0)), _full(gm.shape), _full(gkv.shape),
                  _full(wq.shape), _full(wkv.shape)],
        out_specs=[pl.BlockSpec((tb, wq.shape[1]), lambda i: (i, 0)),
                   pl.BlockSpec((tb, wkv.shape[1]), lambda i: (i, 0))],
        out_shape=[jax.ShapeDtypeStruct((n, wq.shape[1]), F32),
                   jax.ShapeDtypeStruct((n, wkv.shape[1]), F32)],
        compiler_params=_cparams("parallel"),
        name="qkv",
    )(h, gm, gkv, wq, wkv)


def _qkv_strided_kernel(h_ref, gm_ref, gkv_ref, wq_ref, wkv_ref, *refs):
    n_br = len(DILATIONS)
    tail_ref, outs, (q_scr, kv_scr) = refs[0], refs[1:1 + 3 * n_br], refs[1 + 3 * n_br:]
    tb = h_ref.shape[0]
    hw = N_HEADS_B * HEAD_DIM
    tiles = hw // LANES
    h = h_ref[...]
    q = _dot(_rms(h, gm_ref[...]).astype(BF16), wq_ref[...])
    kv = _dot(_rms(h, gkv_ref[...]).astype(BF16), wkv_ref[...])
    tail_ref[...] = kv
    for j in range(q_scr.shape[0]):
        q_scr[j] = q[:, j * LANES:(j + 1) * LANES]
    for j in range(kv_scr.shape[0]):
        kv_scr[j] = kv[:, j * LANES:(j + 1) * LANES]
    for g, r in enumerate(DILATIONS):
        q_ref, k_ref, v_ref = outs[3 * g:3 * g + 3]
        for c in range(r):
            rows = pl.ds(c, tb // r, stride=r)
            for j in range(tiles):
                cols = slice(j * LANES, (j + 1) * LANES)
                q_ref[c, :, cols] = q_scr[g * tiles + j, rows, :].astype(BF16)
                k_ref[c, :, cols] = kv_scr[2 * g * tiles + j, rows, :].astype(BF16)
                v_ref[c, :, cols] = kv_scr[(2 * g + 1) * tiles + j, rows, :].astype(BF16)


def _qkv_strided(h, gm, gkv, wq, wkv, tb, tail_rows):
    n, d = h.shape
    hw = N_HEADS_B * HEAD_DIM
    first_tail_block = (n - tail_rows) // tb
    out_specs = [pl.BlockSpec((tb, wkv.shape[1]), lambda i: (jnp.maximum(i - first_tail_block, 0), 0))]
    out_shape = [jax.ShapeDtypeStruct((tail_rows, wkv.shape[1]), F32)]
    for r in DILATIONS:
        out_specs += [pl.BlockSpec((r, tb // r, hw), lambda i: (0, i, 0))] * 3
        out_shape += [jax.ShapeDtypeStruct((r, n // r, hw), BF16)] * 3
    outs = pl.pallas_call(
        _qkv_strided_kernel,
        grid=(n // tb,),
        in_specs=[pl.BlockSpec((tb, d), lambda i: (i, 0)), _full(gm.shape), _full(gkv.shape),
                  _full(wq.shape), _full(wkv.shape)],
        out_specs=out_specs,
        out_shape=out_shape,
        scratch_shapes=[pltpu.VMEM((wq.shape[1] // LANES, tb, LANES), F32),
                        pltpu.VMEM((wkv.shape[1] // LANES, tb, LANES), F32)],
        compiler_params=_cparams("arbitrary"),
        name="qkv_strided",
    )(h, gm, gkv, wq, wkv)
    return outs[0], [outs[1 + 3 * g:4 + 3 * g] for g in range(len(DILATIONS))]


def _alibi_slopes():
    n = len(WINDOWS) * N_HEADS_B
    e = (np.arange(n, dtype=np.float32) + 1.0) * np.float32(8.0 / n)
    return np.exp2(-e).astype(np.float32).reshape(len(WINDOWS), N_HEADS_B)


def _attn_seq_kernel(q_ref, kc_ref, kp_ref, vc_ref, vp_ref, o_ref, lse_ref, *, dilation, slopes):
    b = pl.program_id(1)
    qi = lax.broadcasted_iota(jnp.int32, (SPAN, SPAN), 0)
    ki = lax.broadcasted_iota(jnp.int32, (SPAN, SPAN), 1)
    dist_c = (qi - ki).astype(F32) * float(dilation)
    dist_p = (qi - ki + SPAN).astype(F32) * float(dilation)
    valid_c = ki <= qi
    valid_p = jnp.logical_and(ki >= qi, b > 0)
    scale = HEAD_DIM ** -0.5
    q, kc, kp, vc, vp = q_ref[...], kc_ref[...], kp_ref[...], vc_ref[...], vp_ref[...]
    heads = [slice(h * HEAD_DIM, (h + 1) * HEAD_DIM) for h in range(N_HEADS_B)]
    raw = [(_dot_nt(q[:, hs], kc[:, hs]), _dot_nt(q[:, hs], kp[:, hs])) for hs in heads]
    probs, outs, lses = [], [], []
    for h, (s_c, s_p) in enumerate(raw):
        s_c = jnp.where(valid_c, s_c * scale - slopes[h] * dist_c, NEG)
        s_p = jnp.where(valid_p, s_p * scale - slopes[h] * dist_p, NEG)
        m = jnp.maximum(jnp.max(s_c, axis=-1, keepdims=True), jnp.max(s_p, axis=-1, keepdims=True))
        e_c = jnp.exp(s_c - m)
        e_p = jnp.exp(s_p - m)
        l = jnp.sum(e_c, axis=-1, keepdims=True) + jnp.sum(e_p, axis=-1, keepdims=True)
        probs.append((e_c.astype(BF16), e_p.astype(BF16), l))
        lses.append(jnp.broadcast_to(m + jnp.log(l), (SPAN, HEAD_DIM)))
    for hs, (e_c, e_p, l) in zip(heads, probs):
        outs.append((_dot(e_c, vc[:, hs]) + _dot(e_p, vp[:, hs])) / l)
    o_ref[...] = jnp.concatenate(outs, axis=1)
    lse_ref[...] = jnp.concatenate(lses, axis=1)


def _attn_seq(q, k, v, branch):
    r, length, hw = q.shape
    cur = pl.BlockSpec((None, SPAN, hw), lambda c, b: (c, b, 0))
    prev = pl.BlockSpec((None, SPAN, hw), lambda c, b: (c, jnp.maximum(b - 1, 0), 0))
    return pl.pallas_call(
        functools.partial(_attn_seq_kernel, dilation=r, slopes=[float(s) for s in _alibi_slopes()[branch]]),
        grid=(r, length // SPAN),
        in_specs=[cur, cur, prev, cur, prev],
        out_specs=[cur, cur],
        out_shape=[jax.ShapeDtypeStruct((r, length, hw), F32)] * 2,
        compiler_params=_cparams("parallel", "arbitrary"),
        name=f"attn_seq_w{WINDOWS[branch]}",
    )(q, k, k, v, v)


STEP_HEAD_GROUPS = 2


def _attn_step_kernel(q_ref, kvn_ref, sl_ref, *refs):
    n_br = len(WINDOWS)
    st_refs, new_refs, (o_ref, lse_ref) = refs[:n_br], refs[n_br:2 * n_br], refs[2 * n_br:]
    rows = q_ref.shape[-2]
    heads = rows // HEAD_DIM
    scale = HEAD_DIM ** -0.5
    per_head = lambda a: a.reshape(heads, HEAD_DIM, a.shape[-1])
    for g in range(n_br):
        w_len = st_refs[g].shape[-1]
        q = per_head(q_ref[0, g, 0])
        k_new, v_new = kvn_ref[0, g, 0, 0], kvn_ref[0, g, 1, 0]
        k_old, v_old = st_refs[g][0, 0, 0], st_refs[g][0, 1, 0]
        slope = per_head(sl_ref[g, 0])[:, 0:1, :]
        pos = lax.broadcasted_iota(jnp.int32, (heads, 1, w_len), 2)
        s_old = jnp.sum(per_head(k_old) * q, axis=1, keepdims=True) * scale - slope * (w_len - pos).astype(F32)
        s_old = jnp.where(lax.rem(pos, DILATIONS[g]) == 0, s_old, NEG)
        s_new = jnp.sum(per_head(k_new) * q, axis=1, keepdims=True) * scale
        m = jnp.maximum(jnp.max(s_old, axis=2, keepdims=True), s_new)
        e_old = jnp.exp(s_old - m)
        e_new = jnp.exp(s_new - m)
        l = jnp.sum(e_old, axis=2, keepdims=True) + e_new
        o = (jnp.sum(e_old * per_head(v_old), axis=2, keepdims=True) + e_new * per_head(v_new)) / l
        o_ref[0, g, 0] = o.reshape(rows, 1)
        lse_ref[0, g, 0] = jnp.broadcast_to(m + jnp.log(l), o.shape).reshape(rows, 1)
        last = lax.broadcasted_iota(jnp.int32, (rows, w_len), 1) == w_len - 1
        for kv, old, new in ((0, k_old, k_new), (1, v_old, v_new)):
            new_refs[g][0, kv, 0] = jnp.where(last, new, pltpu.roll(old, w_len - 1, axis=1))


def _attn_step(q, kv_new, states):
    nb, n_br = q.shape[:2]
    hg = STEP_HEAD_GROUPS
    rows = N_HEADS_B // hg * HEAD_DIM
    slopes = jnp.asarray(np.repeat(_alibi_slopes(), HEAD_DIM, axis=1).reshape(n_br, hg, rows, 1))
    st_args, st_specs = [], []
    for g in range(n_br):
        assert states[g].shape[1] == WINDOWS[g], "window buffers must be full"
        st_args.append(jnp.transpose(states[g], (0, 2, 3, 4, 1)).reshape(nb, 2, hg, rows, WINDOWS[g]))
        st_specs.append(pl.BlockSpec((1, 2, 1, rows, WINDOWS[g]), lambda n, j: (n, 0, j, 0, 0)))
    q = q.reshape(nb, n_br, hg, rows, 1)
    kv_new = kv_new.reshape(nb, n_br, 2, hg, rows, 1)
    vec = pl.BlockSpec((1, n_br, 1, rows, 1), lambda n, j: (n, 0, j, 0, 0))
    outs = pl.pallas_call(
        _attn_step_kernel,
        grid=(nb, hg),
        in_specs=[vec, pl.BlockSpec((1, n_br, 2, 1, rows, 1), lambda n, j: (n, 0, 0, j, 0, 0)),
                  pl.BlockSpec((n_br, 1, rows, 1), lambda n, j: (0, j, 0, 0))] + st_specs,
        out_specs=st_specs + [vec, vec],
        out_shape=[jax.ShapeDtypeStruct(a.shape, F32) for a in st_args] + [jax.ShapeDtypeStruct(q.shape, F32)] * 2,
        compiler_params=_cparams("parallel", "parallel"),
        name="attn_step",
    )(q, kv_new, slopes, *st_args)
    new_states = [jnp.transpose(a.reshape(nb, 2, N_HEADS_B, HEAD_DIM, WINDOWS[g]), (0, 4, 1, 2, 3))
                  for g, a in enumerate(outs[:n_br])]
    o, lse = (a.reshape(nb, n_br, N_HEADS_B * HEAD_DIM) for a in outs[n_br:])
    return o, lse, new_states


def _attn_out_kernel(*refs, dilations):
    n_br = len(dilations)
    o_refs, lse_refs = refs[:n_br], refs[n_br:2 * n_br]
    h_ref, wo_ref, out_ref = refs[2 * n_br:2 * n_br + 3]
    scratch = refs[2 * n_br + 3:]
    tb = h_ref.shape[0]

    def token_order(ref, scr_ref, r):
        if r == 1:
            return ref[0]
        tiles = scr_ref.shape[0]
        for c in range(r):
            for j in range(tiles):
                scr_ref[j, pl.ds(c, tb // r, stride=r), :] = ref[c, :, j * LANES:(j + 1) * LANES]
        return jnp.concatenate([scr_ref[j] for j in range(tiles)], axis=1)

    lses = [token_order(ref, scratch[g], dilations[g]) for g, ref in enumerate(lse_refs)]
    m = functools.reduce(jnp.maximum, lses)
    ws = [jnp.exp(l - m) for l in lses]
    tot = functools.reduce(lambda a, b: a + b, ws)
    o = functools.reduce(lambda a, b: a + b, [w / tot * token_order(ref, scratch[n_br + g], dilations[g])
                                              for g, (w, ref) in enumerate(zip(ws, o_refs))])
    out_ref[...] = h_ref[...] + _dot(o.astype(BF16), wo_ref[...])


def _attn_out(os, lses, h, wo, tb, dilations):
    n, d = h.shape
    hw = wo.shape[0]
    specs = [pl.BlockSpec((r, tb // r, hw), lambda i: (0, i, 0)) for r in dilations]
    return pl.pallas_call(
        functools.partial(_attn_out_kernel, dilations=dilations),
        grid=(n // tb,),
        in_specs=specs + specs + [pl.BlockSpec((tb, d), lambda i: (i, 0)), _full(wo.shape)],
        out_specs=pl.BlockSpec((tb, d), lambda i: (i, 0)),
        out_shape=jax.ShapeDtypeStruct((n, d), F32),
        scratch_shapes=[pltpu.VMEM((hw // LANES, tb, LANES), F32)] * (2 * len(dilations)),
        compiler_params=_cparams("parallel"),
        name="attn_out",
    )(*os, *lses, h, wo)


def kernel(x_prompt, x_sample, state_kv_w128, state_kv_w512, state_kv_w2048, p_prompt, p_sample, g_mix, sgu_w_in, sgu_g_v, sgu_w_s, sgu_b_s, sgu_w_out, kv_g, w_kv, attn_w_q, attn_w_o, g_ffn, peer_w_query, peer_subkeys, peer_u, peer_v, ple_g, ple_w_gate, ple_w_proj, g_final):
    depth, d = g_mix.shape
    assert depth == 2 and sgu_w_in.shape[0] == 1 and attn_w_q.shape[0] == 1
    b_p, s_p, _ = x_prompt.shape
    b_s, t_s, _ = x_sample.shape
    assert b_p == 1 and t_s == 1 and s_p % max(WINDOWS) == 0
    states = (state_kv_w128, state_kv_w512, state_kv_w2048)
    hw = N_HEADS_B * HEAD_DIM
    n_br = len(WINDOWS)

    row = lambda a: a.reshape(1, -1).astype(F32)
    bf = lambda a: a.astype(BF16)

    w_in, w_out = bf(sgu_w_in[0]), bf(sgu_w_out[0])
    gw = d // SGU_GROUPS
    bs_full = jnp.repeat(sgu_b_s[0].T, gw, axis=1)
    ws0 = jnp.repeat(sgu_w_s[0][:, 0, 0], gw).reshape(1, d)
    bs0 = bs_full[0:1]
    wq_peer = bf(peer_w_query)
    subkeys = bf(peer_subkeys.reshape(depth, 2 * PEER_HEADS, PEER_NKEYS, -1))
    u_tab = bf(peer_u)
    v_tab = bf(peer_v)
    w_gate, w_proj = bf(ple_w_gate), bf(ple_w_proj)
    w_q, w_kvb, w_o = bf(attn_w_q[0]), bf(w_kv), bf(attn_w_o[0])

    n_s = LANES
    xs = jnp.pad(x_sample.reshape(b_s, d), ((0, n_s - b_s), (0, 0)))
    ps = jnp.pad(p_sample.reshape(depth, b_s, -1), ((0, 0), (0, n_s - b_s), (0, 0)))
    xp = x_prompt.reshape(s_p, d)
    pp = p_prompt.reshape(depth, s_p, -1)

    def channel(h, p, i, tb_sel, tb, tbp, final):
        h = _peer(h, row(g_ffn[i]), wq_peer[i], subkeys[i], u_tab[i], v_tab[i], tb_sel, tb, 2048)
        return _ple(h, p[i], row(ple_g[i]), w_gate[i], w_proj[i], row(g_final), tbp, final)

    hp = _sgu_seq(xp, row(g_mix[0]), w_in, row(sgu_g_v[0]), sgu_w_s[0], bs_full, w_out, 512)
    hs, v_s = _sgu_first(xs, row(g_mix[0]), w_in, row(sgu_g_v[0]), ws0, bs0, w_out)
    hp = channel(hp, pp, 0, 1024, 256, 512, False)
    hs = channel(hs, ps, 0, n_s, n_s, n_s, False)

    tail = min(max(WINDOWS), s_p)
    kv_tail, qkv_p = _qkv_strided(hp, row(g_mix[1]), row(kv_g), w_q, w_kvb, 512, tail)
    qs, kvs = _qkv(hs, row(g_mix[1]), row(kv_g), w_q, w_kvb, n_s)
    outs = [_attn_seq(*qkv_p[g], g) for g in range(n_br)]
    hp = _attn_out([o for o, _ in outs], [l for _, l in outs], hp, w_o, 512, DILATIONS)
    q_step = qs[:b_s].reshape(b_s, n_br, N_HEADS_B, HEAD_DIM)
    kv_step = kvs[:b_s].reshape(b_s, n_br, 2, N_HEADS_B, HEAD_DIM)
    o_s, lse_s, new_s = _attn_step(q_step, kv_step, states)
    pad = lambda a, g: jnp.pad(a[:, g].reshape(1, b_s, hw), ((0, 0), (0, n_s - b_s), (0, 0)))
    hs = _attn_out([pad(o_s, g) for g in range(n_br)], [pad(lse_s, g) for g in range(n_br)],
                   hs, w_o, n_s, (1,) * n_br)
    yp = channel(hp, pp, 1, 1024, 256, 512, True)
    ys = channel(hs, ps, 1, n_s, n_s, n_s, True)

    kv_p = kv_tail.reshape(1, tail, n_br, 2, N_HEADS_B, HEAD_DIM)
    new_p = [kv_p[:, tail - min(WINDOWS[g], s_p):, g] for g in range(n_br)]
    return (yp.reshape(1, s_p, d), ys[:b_s].reshape(b_s, 1, d), v_s[:b_s].reshape(1, b_s, 1, d),
            new_p[0], new_p[1], new_p[2], new_s[0], new_s[1], new_s[2])
```
